```python
import math
import jax, jax.numpy as jnp
from jax import lax
import numpy as np

D_MODEL = 2048
BATCH = 2
SEQ = 8192
DEPTH = 1
DEC_BATCH = 32
DEC_SEQ = 64
PAST_LEN = 1024

CHUNK = 64
N_HEADS = 8
HEAD_DIM = 128
V_DIM = 2 * HEAD_DIM
ROPE_DIM = HEAD_DIM // 4
ROPE_THETA = 500000.0
Q_BLOCK = 128
D_RNN = D_MODEL
RG_BLOCKS = 16
RG_BW = D_RNN // RG_BLOCKS
CONV_W = 4
RG_C = 8.0
N_KEYS = 128
N_EXPERTS = N_KEYS * N_KEYS
P_HEADS = 8
D_KEY = 256
HALF_KEY = D_KEY // 2
TOPK_HALF = 16
PEER_TOPK = 16
PEER_BLOCK = 128
QK_W = N_HEADS * 2 * HEAD_DIM
V_W = N_HEADS * V_DIM
IN_W = 2 * QK_W + V_W + 2 * D_RNN + 2 * D_MODEL
SPLIT_IDX = (QK_W, 2 * QK_W, 2 * QK_W + V_W, 2 * QK_W + V_W + D_RNN,
             2 * QK_W + V_W + 2 * D_RNN, 2 * QK_W + V_W + 2 * D_RNN + D_MODEL)
ALPHA = (2.0 * DEPTH) ** 0.25
BETA = (8.0 * DEPTH) ** -0.25
LN_EPS = 1e-5
NEG_INF = -1e30
ATTN_SCALE = HEAD_DIM ** -0.5

kernel_name = 'hybrid_diffattn_rglru_peer_stream_step'


def _layer_norm(x, g, b):
    xf = x.astype(jnp.float32)
    mu = jnp.mean(xf, axis=-1, keepdims=True)
    var = jnp.mean(jnp.square(xf - mu), axis=-1, keepdims=True)
    y = (xf - mu) * lax.rsqrt(var + LN_EPS) * g.astype(jnp.float32) + b.astype(jnp.float32)
    return y.astype(x.dtype)


def _partial_rope(t, pos):
    half = ROPE_DIM // 2
    inv = ROPE_THETA ** (-jnp.arange(half, dtype=jnp.float32) / half)
    ang = pos.astype(jnp.float32)[:, None] * inv[None, :]
    cos = jnp.cos(ang)[None, :, None, None, :]
    sin = jnp.sin(ang)[None, :, None, None, :]
    tr = t[..., :ROPE_DIM].astype(jnp.float32)
    t1, t2 = tr[..., :half], tr[..., half:]
    rot = jnp.concatenate([t1 * cos - t2 * sin, t2 * cos + t1 * sin], axis=-1)
    return jnp.concatenate([rot.astype(t.dtype), t[..., ROPE_DIM:]], axis=-1)


def _diff_attn_core(q, k, v, lam, mask):
    s = jnp.einsum('bqhcd,bkhcd->bhcqk', q, k).astype(jnp.float32) * ATTN_SCALE
    if mask is not None:
        s = jnp.where(mask, s, NEG_INF)
    p = jax.nn.softmax(s, axis=-1)
    a = p[:, :, 0] - lam * p[:, :, 1]
    return jnp.einsum('bhqk,bkhe->bqhe', a.astype(v.dtype), v)


def _attn_prompt(q, k, v, lam):
    b, t = q.shape[0], q.shape[1]
    nb = t // Q_BLOCK
    qb = jnp.swapaxes(q.reshape(b, nb, Q_BLOCK, N_HEADS, 2, HEAD_DIM), 0, 1)
    k_chunk = jnp.arange(t) // CHUNK

    def one_block(args):
        q_i, i = args
        q_chunk = (i * Q_BLOCK + jnp.arange(Q_BLOCK)) // CHUNK
        mask = k_chunk[None, :] <= q_chunk[:, None]
        return _diff_attn_core(q_i, k, v, lam, mask)

    o = lax.map(one_block, (qb, jnp.arange(nb)))
    return jnp.swapaxes(o, 0, 1).reshape(b, t, N_HEADS, V_DIM)


def _attn_sample(q, k, v, k_past, v_past, lam):
    k_all = jnp.concatenate([k_past.astype(k.dtype), k], axis=1)
    v_all = jnp.concatenate([v_past.astype(v.dtype), v], axis=1)
    return _diff_attn_core(q, k_all, v_all, lam, None)


def _lin_comb(left, right):
    a1, b1 = left
    a2, b2 = right
    return a1 * a2, a2 * b1 + b2


def _rg_lru(xc, h0, w_rg_a, b_rg_a, w_rg_x, b_rg_x, rg_lambda):
    b, t, _ = xc.shape
    xb = xc.reshape(b, t, RG_BLOCKS, RG_BW)
    r = jax.nn.sigmoid(jnp.einsum('btnd,nde->btne', xb, w_rg_a) + b_rg_a).reshape(b, t, D_RNN)
    gi = jax.nn.sigmoid(jnp.einsum('btnd,nde->btne', xb, w_rg_x) + b_rg_x).reshape(b, t, D_RNN)
    log_a = -RG_C * r.astype(jnp.float32) * jax.nn.softplus(-rg_lambda.astype(jnp.float32))
    a = jnp.exp(log_a)
    mult = jnp.sqrt(-jnp.expm1(2.0 * log_a))
    u = mult * (gi * xc).astype(jnp.float32)
    u = u.at[:, 0].add(a[:, 0] * h0.astype(jnp.float32))
    _, h = lax.associative_scan(_lin_comb, (a, u), axis=1)
    return h.astype(xc.dtype), h[:, -1].astype(xc.dtype)


def _peer(x, w_query, sub_keys1, sub_keys2, peer_u, peer_v):
    b, t, d = x.shape
    n = b * t
    pad = (-n) % PEER_BLOCK
    xt = jnp.pad(x.reshape(n, d), ((0, pad), (0, 0))).reshape(-1, PEER_BLOCK, d)

    def one_block(xi):
        q = (xi @ w_query).reshape(PEER_BLOCK, P_HEADS, 2, HALF_KEY)
        s1 = jnp.einsum('thd,nd->thn', q[:, :, 0], sub_keys1).astype(jnp.float32)
        s2 = jnp.einsum('thd,nd->thn', q[:, :, 1], sub_keys2).astype(jnp.float32)
        v1, i1 = lax.top_k(s1, TOPK_HALF)
        v2, i2 = lax.top_k(s2, TOPK_HALF)
        n_cand = TOPK_HALF * TOPK_HALF
        cand = (v1[..., :, None] + v2[..., None, :]).reshape(PEER_BLOCK, P_HEADS, n_cand)
        cand_idx = (i1[..., :, None] * N_KEYS + i2[..., None, :]).reshape(PEER_BLOCK, P_HEADS, n_cand)
        top, sel = lax.top_k(cand, PEER_TOPK)
        e_idx = jnp.take_along_axis(cand_idx, sel, axis=-1)
        g = jax.nn.softmax(top, axis=-1).astype(xi.dtype)
        act = jax.nn.gelu(jnp.einsum('thkd,td->thk', peer_u[e_idx], xi))
        return jnp.einsum('thk,thkd->td', g * act, peer_v[e_idx])

    out = lax.map(one_block, xt).reshape(-1, d)[:n]
    return out.reshape(b, t, d)


def _layer(x, pos, k_past, v_past, conv_buf, h0, lam_init, w):
    (w_in, b_in, lambda_q1, lambda_k1, lambda_q2, lambda_k2, subln_g, w_attn_branch,
     conv_w, conv_b, w_rg_a, b_rg_a, w_rg_x, b_rg_x, rg_lambda, w_rec_branch, w_out,
     ln1_g, ln1_b, w_query, sub_keys1, sub_keys2, peer_u, peer_v, ln2_g, ln2_b) = w
    b, t, _ = x.shape
    z = x @ w_in + b_in
    q, k, v, xr, xg, ga, gr = jnp.split(z, SPLIT_IDX, axis=-1)
    q = _partial_rope(q.reshape(b, t, N_HEADS, 2, HEAD_DIM), pos)
    k = _partial_rope(k.reshape(b, t, N_HEADS, 2, HEAD_DIM), pos)
    v = v.reshape(b, t, N_HEADS, V_DIM)
    lam = (jnp.exp(jnp.sum(lambda_q1.astype(jnp.float32) * lambda_k1.astype(jnp.float32)))
           - jnp.exp(jnp.sum(lambda_q2.astype(jnp.float32) * lambda_k2.astype(jnp.float32)))
           + lam_init)
    if k_past is None:
        o = _attn_prompt(q, k, v, lam)
    else:
        o = _attn_sample(q, k, v, k_past, v_past, lam)
    of = o.astype(jnp.float32)
    of = (of * lax.rsqrt(jnp.mean(jnp.square(of), axis=-1, keepdims=True) + LN_EPS)
          * subln_g.astype(jnp.float32) * (1.0 - lam_init))
    branch_a = of.astype(x.dtype).reshape(b, t, V_W) @ w_attn_branch

    xpad = jnp.concatenate([conv_buf.astype(xr.dtype), xr], axis=1)
    xc = conv_b + xpad[:, 0:t] * conv_w[0]
    for j in range(1, CONV_W):
        xc = xc + xpad[:, j:j + t] * conv_w[j]
    conv_new = xpad[:, t:]
    h, h_last = _rg_lru(xc, h0, w_rg_a, b_rg_a, w_rg_x, b_rg_x, rg_lambda)
    branch_b = (h * jax.nn.gelu(xg)) @ w_rec_branch

    m = jax.nn.sigmoid(ga) * branch_a + jax.nn.sigmoid(gr) * branch_b
    x = _layer_norm(ALPHA * x + m @ w_out, ln1_g, ln1_b)
    x = _layer_norm(ALPHA * x + _peer(x, w_query, sub_keys1, sub_keys2, peer_u, peer_v), ln2_g, ln2_b)
    return x, k, v, conv_new, h_last


def setup_inputs(seed: int = 0) -> dict:
    key = jax.random.key(seed)
    ks = jax.random.split(key, 32)
    f32 = jnp.float32

    def nrm(k, shape, scale):
        return jax.random.normal(k, shape, f32) * scale

    x_prompt = nrm(ks[0], (BATCH, SEQ, D_MODEL), 1.0)
    x_sample = nrm(ks[1], (DEC_BATCH, DEC_SEQ, D_MODEL), 1.0)
    cache_k = nrm(ks[2], (DEPTH, DEC_BATCH, PAST_LEN, N_HEADS, 2, HEAD_DIM), 1.0)
    cache_v = nrm(ks[3], (DEPTH, DEC_BATCH, PAST_LEN, N_HEADS, V_DIM), 1.0)
    state_conv = nrm(ks[4], (DEPTH, DEC_BATCH, CONV_W - 1, D_RNN), 1.0)
    state_h = nrm(ks[5], (DEPTH, DEC_BATCH, D_RNN), 0.5)
    w_in = nrm(ks[6], (DEPTH, D_MODEL, IN_W), D_MODEL ** -0.5)
    b_in = nrm(ks[7], (DEPTH, IN_W), 0.02)
    lambda_q1 = nrm(ks[8], (DEPTH, HEAD_DIM), 0.1)
    lambda_k1 = nrm(ks[9], (DEPTH, HEAD_DIM), 0.1)
    lambda_q2 = nrm(ks[10], (DEPTH, HEAD_DIM), 0.1)
    lambda_k2 = nrm(ks[11], (DEPTH, HEAD_DIM), 0.1)
    subln_g = 1.0 + nrm(ks[12], (DEPTH, V_DIM), 0.05)
    w_attn_branch = nrm(ks[13], (DEPTH, V_W, D_MODEL), BETA * V_W ** -0.5)
    conv_w = nrm(ks[14], (DEPTH, CONV_W, D_RNN), CONV_W ** -0.5)
    conv_b = nrm(ks[15], (DEPTH, D_RNN), 0.02)
    w_rg_a = nrm(ks[16], (DEPTH, RG_BLOCKS, RG_BW, RG_BW), RG_BW ** -0.5)
    b_rg_a = nrm(ks[17], (DEPTH, RG_BLOCKS, RG_BW), 0.02)
    w_rg_x = nrm(ks[18], (DEPTH, RG_BLOCKS, RG_BW, RG_BW), RG_BW ** -0.5)
    b_rg_x = nrm(ks[19], (DEPTH, RG_BLOCKS, RG_BW), 0.02)
    a_c = jax.random.uniform(ks[20], (DEPTH, D_RNN), f32, minval=0.9, maxval=0.999)
    p = a_c ** (1.0 / RG_C)
    rg_lambda = jnp.log(p) - jnp.log1p(-p)
    w_rec_branch = nrm(ks[21], (DEPTH, D_RNN, D_MODEL), BETA * D_RNN ** -0.5)
    w_out = nrm(ks[22], (DEPTH, D_MODEL, D_MODEL), BETA * D_MODEL ** -0.5)
    ln1_g = 1.0 + nrm(ks[23], (DEPTH, D_MODEL), 0.05)
    ln1_b = nrm(ks[24], (DEPTH, D_MODEL), 0.02)
    w_query = nrm(ks[25], (DEPTH, D_MODEL, P_HEADS * D_KEY), D_MODEL ** -0.5)
    sub_keys1 = nrm(ks[26], (DEPTH, N_KEYS, HALF_KEY), HALF_KEY ** -0.5)
    sub_keys2 = nrm(ks[27], (DEPTH, N_KEYS, HALF_KEY), HALF_KEY ** -0.5)
    peer_u = nrm(ks[28], (DEPTH, N_EXPERTS, D_MODEL), D_MODEL ** -0.5)
    peer_v = nrm(ks[29], (DEPTH, N_EXPERTS, D_MODEL), BETA)
    ln2_g = 1.0 + nrm(ks[30], (DEPTH, D_MODEL), 0.05)
    ln2_b = nrm(ks[31], (DEPTH, D_MODEL), 0.02)
    return {'x_prompt': x_prompt, 'x_sample': x_sample, 'cache_k': cache_k, 'cache_v': cache_v,
            'state_conv': state_conv, 'state_h': state_h, 'w_in': w_in, 'b_in': b_in,
            'lambda_q1': lambda_q1, 'lambda_k1': lambda_k1, 'lambda_q2': lambda_q2,
            'lambda_k2': lambda_k2, 'subln_g': subln_g, 'w_attn_branch': w_attn_branch,
            'conv_w': conv_w, 'conv_b': conv_b, 'w_rg_a': w_rg_a, 'b_rg_a': b_rg_a,
            'w_rg_x': w_rg_x, 'b_rg_x': b_rg_x, 'rg_lambda': rg_lambda,
            'w_rec_branch': w_rec_branch, 'w_out': w_out, 'ln1_g': ln1_g, 'ln1_b': ln1_b,
            'w_query': w_query, 'sub_keys1': sub_keys1, 'sub_keys2': sub_keys2,
            'peer_u': peer_u, 'peer_v': peer_v, 'ln2_g': ln2_g, 'ln2_b': ln2_b}


def reference(x_prompt, x_sample, cache_k, cache_v, state_conv, state_h, w_in, b_in,
              lambda_q1, lambda_k1, lambda_q2, lambda_k2, subln_g, w_attn_branch,
              conv_w, conv_b, w_rg_a, b_rg_a, w_rg_x, b_rg_x, rg_lambda, w_rec_branch,
              w_out, ln1_g, ln1_b, w_query, sub_keys1, sub_keys2, peer_u, peer_v,
              ln2_g, ln2_b):
    layer_weights = (w_in, b_in, lambda_q1, lambda_k1, lambda_q2, lambda_k2, subln_g,
                     w_attn_branch, conv_w, conv_b, w_rg_a, b_rg_a, w_rg_x, b_rg_x, rg_lambda,
                     w_rec_branch, w_out, ln1_g, ln1_b, w_query, sub_keys1, sub_keys2,
                     peer_u, peer_v, ln2_g, ln2_b)
    past_len = cache_k.shape[2]
    pos_p = jnp.arange(x_prompt.shape[1])
    pos_s = past_len + jnp.arange(x_sample.shape[1])
    conv0 = jnp.zeros((x_prompt.shape[0], CONV_W - 1, D_RNN), x_prompt.dtype)
    h_zero = jnp.zeros((x_prompt.shape[0], D_RNN), x_prompt.dtype)
    yp, ys = x_prompt, x_sample
    kp_l, vp_l, cp_l, hp_l, ks_l, vs_l, cs_l, hs_l = [], [], [], [], [], [], [], []
    for l in range(DEPTH):
        lam_init = 0.8 - 0.6 * math.exp(-0.3 * l)
        wl = tuple(wt[l] for wt in layer_weights)
        yp, kp, vp, cp, hp = _layer(yp, pos_p, None, None, conv0, h_zero, lam_init, wl)
        ys, ks_, vs_, cs_, hs_ = _layer(ys, pos_s, cache_k[l], cache_v[l], state_conv[l],
                                        state_h[l], lam_init, wl)
        kp_l.append(kp); vp_l.append(vp); cp_l.append(cp); hp_l.append(hp)
        ks_l.append(ks_); vs_l.append(vs_); cs_l.append(cs_); hs_l.append(hs_)
    return (yp, ys, jnp.stack(kp_l), jnp.stack(vp_l), jnp.stack(cp_l), jnp.stack(hp_l),
            jnp.stack(ks_l), jnp.stack(vs_l), jnp.stack(cs_l), jnp.stack(hs_l))
```

```python
import functools
import math

import jax
import jax.numpy as jnp
from jax import lax
from jax.experimental import pallas as pl
from jax.experimental.pallas import tpu as pltpu

F32 = jnp.float32
BF16 = jnp.bfloat16

CHUNK = 64
ROPE_THETA = 500000.0
RG_C = 8.0
TOPK = 16
LN_EPS = 1e-5
NEG_INF = -1e30
LOG2E = 1.4426950408889634

LANES = 128
VMEM_LIMIT_BYTES = 56 * 1024 * 1024


def _params(*sem):
    return pltpu.CompilerParams(dimension_semantics=sem, vmem_limit_bytes=VMEM_LIMIT_BYTES)


def _resident(shape, index_map):
    return pl.BlockSpec(shape, index_map, pipeline_mode=pl.Buffered(1))


def _dot(a, b):
    return jnp.dot(a, b, preferred_element_type=F32)


def _dot_nt(a, b):
    return lax.dot_general(a, b, (((1,), (1,)), ((), ())), preferred_element_type=F32)


def _proj_kernel(*refs, rope_shift, emit_f32, emit_bf16):
    it = iter(refs)
    x_ref, w_ref, b_ref = next(it), next(it), next(it)
    if rope_shift:
        c_ref, sa_ref, sb_ref = next(it), next(it), next(it)
    of_ref = next(it) if emit_f32 else None
    ob_ref = next(it) if emit_bf16 else None
    xb_ref = next(it)

    @pl.when(pl.program_id(1) == 0)
    def _():
        xb_ref[...] = x_ref[...].astype(BF16)

    z = _dot(xb_ref[...], w_ref[...]) + b_ref[...]
    if rope_shift:
        c, sa, sb = c_ref[...], sa_ref[...], sb_ref[...]
        parts = []
        for s in range(z.shape[1] // LANES):
            t = z[:, s * LANES:(s + 1) * LANES]
            parts.append(t * c + pltpu.roll(t, rope_shift, 1) * sa
                         + pltpu.roll(t, LANES - rope_shift, 1) * sb)
        z = jnp.concatenate(parts, axis=1) if len(parts) > 1 else parts[0]
    if emit_f32:
        of_ref[...] = z
    if emit_bf16:
        ob_ref[...] = z.astype(BF16)


def _project(x2d, w, b, *, rope=None, emit_f32=True, emit_bf16=False):
    n, kdim = x2d.shape
    nout = w.shape[1]
    tm = min(1024, n)
    tn = min(512, nout)
    in_specs = [pl.BlockSpec((tm, kdim), lambda i, j: (i, 0)),
                pl.BlockSpec((kdim, tn), lambda i, j: (0, j)),
                pl.BlockSpec((1, tn), lambda i, j: (0, j))]
    args = [x2d, w, b]
    rope_shift = 0
    if rope is not None:
        rope_shift, tabs = rope
        in_specs += [pl.BlockSpec((tm, LANES), lambda i, j: (i, 0))] * 3
        args += list(tabs)
    out_shape, out_specs = [], []
    if emit_f32:
        out_shape.append(jax.ShapeDtypeStruct((n, nout), F32))
        out_specs.append(pl.BlockSpec((tm, tn), lambda i, j: (i, j)))
    if emit_bf16:
        out_shape.append(jax.ShapeDtypeStruct((n, nout), BF16))
        out_specs.append(pl.BlockSpec((tm, tn), lambda i, j: (i, j)))
    return pl.pallas_call(
        functools.partial(_proj_kernel, rope_shift=rope_shift, emit_f32=emit_f32, emit_bf16=emit_bf16),
        grid=(n // tm, nout // tn),
        in_specs=in_specs, out_specs=out_specs, out_shape=out_shape,
        scratch_shapes=[pltpu.VMEM((tm, kdim), BF16)],
        compiler_params=_params("parallel", "arbitrary"),
        name="in_proj",
    )(*args)


def _rope_tables(pos, batch, head_dim):
    rope_dim = head_dim // 4
    half = rope_dim // 2
    inv = ROPE_THETA ** (-jnp.arange(half, dtype=F32) / half)
    ang = pos.astype(F32)[:, None] * inv[None, :]
    cos, sin = jnp.cos(ang), jnp.sin(ang)
    t = pos.shape[0]
    ones = jnp.ones((t, head_dim - rope_dim), F32)
    zeros_h = jnp.zeros((t, half), F32)
    zeros_r = jnp.zeros((t, head_dim - rope_dim), F32)
    c = jnp.concatenate([cos, cos, ones], axis=1)
    sa = jnp.concatenate([zeros_h, sin, zeros_r], axis=1)
    sb = jnp.concatenate([-sin, zeros_h, zeros_r], axis=1)
    return half, tuple(jnp.tile(a, (batch, 1)) for a in (c, sa, sb))


def _lambda_value(lam_ref, lam_init):
    lv = lam_ref[...]
    s1 = jnp.sum(lv[0:1] * lv[1:2], axis=1, keepdims=True)
    s2 = jnp.sum(lv[2:3] * lv[3:4], axis=1, keepdims=True)
    return jnp.exp(s1) - jnp.exp(s2) + lam_init


def _sub_norm(o, g, lam_init):
    ms = jnp.mean(o * o, axis=1, keepdims=True)
    return o * lax.rsqrt(ms + LN_EPS) * g * (1.0 - lam_init)


def _attn_prompt_kernel(lam_ref, g_ref, q_ref, k_ref, v_ref, o_ref, acc1_ref, acc2_ref,
                        *, tq, hd, lam_init, scale):
    i = pl.program_id(2)
    q = q_ref[0]
    qs = (q[:, :hd], q[:, hd:])
    accs = (acc1_ref, acc2_ref)
    c = scale * LOG2E

    def step(j, carry, masked):
        off = pl.multiple_of(j * tq, tq)
        k = k_ref[0, pl.ds(off, tq), :]
        v = v_ref[0, pl.ds(off, tq), :]
        if masked:
            q_chunk = lax.broadcasted_iota(jnp.int32, (tq, tq), 0) // CHUNK
            k_chunk = lax.broadcasted_iota(jnp.int32, (tq, tq), 1) // CHUNK
            keep = k_chunk <= q_chunk
        new = []
        for comp in range(2):
            m, l = carry[2 * comp], carry[2 * comp + 1]
            s = _dot_nt(qs[comp], k[:, comp * hd:(comp + 1) * hd])
            if masked:
                s = jnp.where(keep, s, NEG_INF)
            m_new = jnp.maximum(m, jnp.max(s, axis=1, keepdims=True))
            alpha = jnp.exp2((m - m_new) * c)
            p = jnp.exp2((s - m_new) * c)
            l_new = alpha * l + jnp.sum(p, axis=1, keepdims=True)
            accs[comp][...] = alpha * accs[comp][...] + _dot(p.astype(BF16), v)
            new += [m_new, l_new]
        return tuple(new)

    acc1_ref[...] = jnp.zeros_like(acc1_ref)
    acc2_ref[...] = jnp.zeros_like(acc2_ref)
    m0 = jnp.full((tq, 1), NEG_INF, F32)
    l0 = jnp.zeros((tq, 1), F32)
    carry = lax.fori_loop(0, i, lambda j, cr: step(j, cr, False), (m0, l0, m0, l0))
    _, l1, _, l2 = step(i, carry, True)

    lam = _lambda_value(lam_ref, lam_init)
    o = acc1_ref[...] / l1 - lam * (acc2_ref[...] / l2)
    o_ref[0] = _sub_norm(o, g_ref[...], lam_init).astype(o_ref.dtype)


def _attn_prompt(q, k, v, lam_vec, g, lam_init, n_heads, hd):
    b, t, _ = q.shape
    vd = v.shape[2] // n_heads
    tq = min(256, t)
    kern = functools.partial(_attn_prompt_kernel, tq=tq, hd=hd, lam_init=lam_init, scale=hd ** -0.5)
    return pl.pallas_call(
        kern,
        grid=(b, n_heads, t // tq),
        in_specs=[_resident((4, hd), lambda bi, h, i: (0, 0)),
                  _resident((1, vd), lambda bi, h, i: (0, 0)),
                  pl.BlockSpec((1, tq, 2 * hd), lambda bi, h, i: (bi, i, h)),
                  pl.BlockSpec((1, t, 2 * hd), lambda bi, h, i: (bi, 0, h)),
                  pl.BlockSpec((1, t, vd), lambda bi, h, i: (bi, 0, h))],
        out_specs=pl.BlockSpec((1, tq, vd), lambda bi, h, i: (bi, i, h)),
        out_shape=jax.ShapeDtypeStruct((b, t, n_heads * vd), BF16),
        scratch_shapes=[pltpu.VMEM((tq, vd), F32), pltpu.VMEM((tq, vd), F32)],
        compiler_params=_params("parallel", "parallel", "arbitrary"),
        name="attn_prompt",
    )(lam_vec, g, q, k, v)


def _attn_sample_kernel(lam_ref, g_ref, q_ref, kn_ref, vn_ref, kp_ref, vp_ref, o_ref,
                        *, hd, lam_init, scale):
    q, kn, vn = q_ref[0], kn_ref[0], vn_ref[0]
    kp = kp_ref[0].astype(BF16)
    vp = vp_ref[0].astype(BF16)
    c = scale * LOG2E
    outs = []
    for comp in range(2):
        lo, hi = comp * hd, (comp + 1) * hd
        sp = _dot_nt(q[:, lo:hi], kp[:, lo:hi])
        sn = _dot_nt(q[:, lo:hi], kn[:, lo:hi])
        m = jnp.maximum(jnp.max(sp, axis=1, keepdims=True), jnp.max(sn, axis=1, keepdims=True))
        pp = jnp.exp2((sp - m) * c)
        pn = jnp.exp2((sn - m) * c)
        l = jnp.sum(pp, axis=1, keepdims=True) + jnp.sum(pn, axis=1, keepdims=True)
        outs.append((_dot(pp.astype(BF16), vp) + _dot(pn.astype(BF16), vn)) / l)
    lam = _lambda_value(lam_ref, lam_init)
    o = outs[0] - lam * outs[1]
    o_ref[0] = _sub_norm(o, g_ref[...], lam_init).astype(o_ref.dtype)


def _attn_sample(q, k, v, k_past, v_past, lam_vec, g, lam_init, n_heads, hd):
    b, t, _ = q.shape
    p = k_past.shape[1]
    vd = v.shape[2] // n_heads
    kern = functools.partial(_attn_sample_kernel, hd=hd, lam_init=lam_init, scale=hd ** -0.5)
    return pl.pallas_call(
        kern,
        grid=(b, n_heads),
        in_specs=[_resident((4, hd), lambda bi, h: (0, 0)),
                  _resident((1, vd), lambda bi, h: (0, 0)),
                  pl.BlockSpec((1, t, 2 * hd), lambda bi, h: (bi, 0, h)),
                  pl.BlockSpec((1, t, 2 * hd), lambda bi, h: (bi, 0, h)),
                  pl.BlockSpec((1, t, vd), lambda bi, h: (bi, 0, h)),
                  pl.BlockSpec((1, p, 2 * hd), lambda bi, h: (bi, 0, h)),
                  pl.BlockSpec((1, p, vd), lambda bi, h: (bi, 0, h))],
        out_specs=pl.BlockSpec((1, t, vd), lambda bi, h: (bi, 0, h)),
        out_shape=jax.ShapeDtypeStruct((b, t, n_heads * vd), BF16),
        compiler_params=_params("parallel", "parallel"),
        name="attn_sample",
    )(lam_vec, g, q, k, v, k_past, v_past)


def _recurrent_kernel(xr_ref, xg_ref, cbuf_ref, h0_ref, cw_ref, cb_ref, wrg_ref, bra_ref, brx_ref,
                      lam_ref, y_ref, cnew_ref, hlast_ref, xpad_ref, a_ref, u_ref, hs_ref, hc_ref,
                      *, tt, cw, bw):
    i = pl.program_id(1)
    hdr = 8
    d = xr_ref.shape[2]

    @pl.when(i == 0)
    def _():
        xpad_ref[hdr - (cw - 1):hdr, :] = cbuf_ref[0]
        hc_ref[...] = h0_ref[0]

    xpad_ref[hdr:hdr + tt, :] = xr_ref[0]
    xc = cb_ref[...] + xpad_ref[hdr - (cw - 1):hdr - (cw - 1) + tt, :] * cw_ref[0:1, :]
    for j in range(1, cw):
        lo = hdr - (cw - 1) + j
        xc = xc + xpad_ref[lo:lo + tt, :] * cw_ref[j:j + 1, :]
    tail = xpad_ref[hdr + tt - (cw - 1):hdr + tt, :]
    cnew_ref[0] = tail
    xpad_ref[hdr - (cw - 1):hdr, :] = tail

    for n in range(d // bw):
        lo, hi = n * bw, (n + 1) * bw
        xcn = xc[:, lo:hi]
        gates = _dot(xcn.astype(BF16), wrg_ref[n])
        r = jax.nn.sigmoid(gates[:, :bw] + bra_ref[:, lo:hi])
        gi = jax.nn.sigmoid(gates[:, bw:] + brx_ref[:, lo:hi])
        log_a = -RG_C * r * jax.nn.softplus(-lam_ref[:, lo:hi])
        a = jnp.exp(log_a)
        a_ref[:, lo:hi] = a
        u_ref[:, lo:hi] = jnp.sqrt(-jnp.tanh(log_a) * (a * a + 1.0)) * (gi * xcn)

    def scan_row(t, h):
        h = a_ref[pl.ds(t, 1), :] * h + u_ref[pl.ds(t, 1), :]
        hs_ref[pl.ds(t, 1), :] = h
        return h

    h = lax.fori_loop(0, tt, scan_row, hc_ref[...])
    hc_ref[...] = h
    hlast_ref[0] = h
    y_ref[0] = (hs_ref[...] * jax.nn.gelu(xg_ref[0])).astype(y_ref.dtype)


def _recurrent(zrest, b, t, d, conv_buf, h0, conv_w, conv_b, w_rg, b_rg_a, b_rg_x, rg_lambda):
    cw = conv_w.shape[0]
    bw = w_rg.shape[1]
    tt = min(256, t)
    z3 = zrest.reshape(b, t, zrest.shape[1])
    kern = functools.partial(_recurrent_kernel, tt=tt, cw=cw, bw=bw)
    row = lambda bi, i: (0, 0)
    y, cnew, hlast = pl.pallas_call(
        kern,
        grid=(b, t // tt),
        in_specs=[pl.BlockSpec((1, tt, d), lambda bi, i: (bi, i, 0)),
                  pl.BlockSpec((1, tt, d), lambda bi, i: (bi, i, 1)),
                  pl.BlockSpec((1, cw - 1, d), lambda bi, i: (bi, 0, 0)),
                  pl.BlockSpec((1, 1, d), lambda bi, i: (bi, 0, 0)),
                  _resident((cw, d), row),
                  _resident((1, d), row),
                  _resident(w_rg.shape, lambda bi, i: (0, 0, 0)),
                  _resident((1, d), row),
                  _resident((1, d), row),
                  _resident((1, d), row)],
        out_specs=[pl.BlockSpec((1, tt, d), lambda bi, i: (bi, i, 0)),
                   pl.BlockSpec((1, cw - 1, d), lambda bi, i: (bi, 0, 0)),
                   pl.BlockSpec((1, 1, d), lambda bi, i: (bi, 0, 0))],
        out_shape=[jax.ShapeDtypeStruct((b, t, d), BF16),
                   jax.ShapeDtypeStruct((b, cw - 1, d), F32),
                   jax.ShapeDtypeStruct((b, 1, d), F32)],
        scratch_shapes=[pltpu.VMEM((tt + 8, d), F32), pltpu.VMEM((tt, d), F32),
                        pltpu.VMEM((tt, d), F32), pltpu.VMEM((tt, d), F32), pltpu.VMEM((1, d), F32)],
        compiler_params=_params("parallel", "arbitrary"),
        name="recurrent",
    )(z3, z3, conv_buf, h0.reshape(b, 1, d), conv_w, conv_b, w_rg, b_rg_a, b_rg_x, rg_lambda)
    return y, cnew, hlast.reshape(b, d)


def _layer_norm(x, g, b):
    mu = jnp.mean(x, axis=1, keepdims=True)
    xc = x - mu
    var = jnp.mean(xc * xc, axis=1, keepdims=True)
    return xc * lax.rsqrt(var + LN_EPS) * g + b


def _merge_kernel(oa_ref, yb_ref, ga_ref, gr_ref, x_ref, wa_ref, wb_ref, wo_ref, g_ref, b_ref,
                  x1_ref, *, alpha):
    br_a = _dot(oa_ref[...], wa_ref[...])
    br_b = _dot(yb_ref[...], wb_ref[...])
    m = jax.nn.sigmoid(ga_ref[...]) * br_a + jax.nn.sigmoid(gr_ref[...]) * br_b
    r = _dot(m.astype(BF16), wo_ref[...])
    x1_ref[...] = _layer_norm(alpha * x_ref[...] + r, g_ref[...], b_ref[...])


def _merge(oa, yb, zrest, x2d, w_attn, w_rec, w_out, ln_g, ln_b, alpha):
    n, d = x2d.shape
    tm = min(256, n)
    tok = lambda c: pl.BlockSpec((tm, d), lambda i: (i, c))
    const = lambda i: (0, 0)
    return pl.pallas_call(
        functools.partial(_merge_kernel, alpha=alpha),
        grid=(n // tm,),
        in_specs=[tok(0), tok(0), tok(2), tok(3), tok(0),
                  _resident(w_attn.shape, const), _resident(w_rec.shape, const),
                  _resident(w_out.shape, const), _resident((1, d), const), _resident((1, d), const)],
        out_specs=tok(0),
        out_shape=jax.ShapeDtypeStruct((n, d), F32),
        compiler_params=_params("parallel"),
        name="merge",
    )(oa, yb, zrest, zrest, x2d, w_attn, w_rec, w_out, ln_g, ln_b)


def _top_rows(x, k, fill):
    rows = []
    for it in range(k):
        m = jnp.max(x, axis=0, keepdims=True)
        rows.append(m)
        if it + 1 < k:
            x = jnp.where(x == m, fill, x)
    return rows


def _candidate_pairs(k):
    return [(a, b) for a in range(k) for b in range(k) if (a + 1) * (b + 1) <= k]


def _peer_route_kernel(xt_ref, wq_ref, sk1_ref, sk2_ref, e1_ref, e2_ref, pt_ref, *, heads, half):
    qt = _dot(wq_ref[...], xt_ref[...]).astype(BF16)
    tm = qt.shape[1]
    pairs = _candidate_pairs(TOPK)
    pad_rows = (-len(pairs)) % 8
    pad = [jnp.full((pad_rows, tm), -1.0, F32)] if pad_rows else []

    def pair_products(ea, eb):
        return jnp.concatenate([ea[a] * eb[b] for a, b in pairs] + pad, axis=0)

    for h in range(heads):
        base = h * 2 * half
        s1 = _dot(sk1_ref[...], qt[base:base + half])
        s2 = _dot(sk2_ref[...], qt[base + half:base + 2 * half])
        v1 = _top_rows(s1, TOPK, -jnp.inf)
        v2 = _top_rows(s2, TOPK, -jnp.inf)
        e1 = jnp.exp(s1 - v1[0])
        e2 = jnp.exp(s2 - v2[0])
        e1_top = [jnp.exp(v - v1[0]) for v in v1]
        e2_top = [jnp.exp(v - v2[0]) for v in v2]
        tops = _top_rows(pair_products(e1_top, e2_top), TOPK, -1.0)
        z = functools.reduce(lambda acc, r: acc + jnp.maximum(r, 0.0), tops[1:], tops[0])
        rz = 1.0 / z
        e1n = e1 * rz
        e1n_top = [e * rz for e in e1_top]
        thr = _top_rows(pair_products(e1n_top, e2_top), TOPK, -1.0)[-1]
        e1_ref[h] = e1n
        e2_ref[h] = e2
        pt_ref[h:h + 1, :] = jnp.maximum(thr, 0.0)


def _peer_route(x1t, wq_t, sk1, sk2, heads):
    d, n = x1t.shape
    nk, half = sk1.shape
    tm = min(512, n)
    const = lambda i: (0, 0)
    return pl.pallas_call(
        functools.partial(_peer_route_kernel, heads=heads, half=half),
        grid=(n // tm,),
        in_specs=[pl.BlockSpec((d, tm), lambda i: (0, i)),
                  _resident(wq_t.shape, const), _resident(sk1.shape, const), _resident(sk2.shape, const)],
        out_specs=[pl.BlockSpec((heads, nk, tm), lambda i: (0, 0, i)),
                   pl.BlockSpec((heads, nk, tm), lambda i: (0, 0, i)),
                   pl.BlockSpec((heads, tm), lambda i: (0, i))],
        out_shape=[jax.ShapeDtypeStruct((heads, nk, n), F32),
                   jax.ShapeDtypeStruct((heads, nk, n), F32),
                   jax.ShapeDtypeStruct((heads, n), F32)],
        compiler_params=_params("parallel"),
        name="peer_route",
    )(x1t, wq_t, sk1, sk2)


def _peer_dense_kernel(xt_ref, u_ref, vt_ref, e1_ref, e2_ref, pt_ref, o_ref, *, heads, nk, rows_per_step):
    j = pl.program_id(1)

    @pl.when(j == 0)
    def _():
        o_ref[...] = jnp.zeros_like(o_ref)

    act = jax.nn.gelu(_dot(u_ref[...], xt_ref[...]))
    parts = []
    for a in range(rows_per_step):
        w = None
        for h in range(heads):
            p = e1_ref[h, a:a + 1, :] * e2_ref[h]
            sel = jnp.where(p >= pt_ref[h:h + 1, :], p, 0.0)
            w = sel if w is None else w + sel
        parts.append((w * act[a * nk:(a + 1) * nk]).astype(BF16))
    wa = jnp.concatenate(parts, axis=0) if len(parts) > 1 else parts[0]
    o_ref[...] += _dot(vt_ref[...], wa)


def _peer_dense(x1t, u, vt, e1, e2, pt):
    d, n = x1t.shape
    heads, nk, _ = e1.shape
    ne = u.shape[0]
    tm = min(512, n)
    rows_per_step = 8
    ec = rows_per_step * nk
    kern = functools.partial(_peer_dense_kernel, heads=heads, nk=nk, rows_per_step=rows_per_step)
    return pl.pallas_call(
        kern,
        grid=(n // tm, ne // ec),
        in_specs=[_resident((d, tm), lambda i, j: (0, i)),
                  pl.BlockSpec((ec, d), lambda i, j: (j, 0)),
                  pl.BlockSpec((d, ec), lambda i, j: (0, j)),
                  pl.BlockSpec((heads, rows_per_step, tm), lambda i, j: (0, j, i)),
                  _resident((heads, nk, tm), lambda i, j: (0, 0, i)),
                  _resident((heads, tm), lambda i, j: (0, i))],
        out_specs=pl.BlockSpec((d, tm), lambda i, j: (0, i)),
        out_shape=jax.ShapeDtypeStruct((d, n), F32),
        compiler_params=_params("parallel", "arbitrary"),
        name="peer_dense",
    )(x1t, u, vt, e1, e2, pt)


def _final_norm_kernel(x_ref, p_ref, g_ref, b_ref, y_ref, *, alpha):
    y_ref[...] = _layer_norm(alpha * x_ref[...] + p_ref[...], g_ref[...], b_ref[...])


def _final_norm(x1, peer_out, ln_g, ln_b, alpha):
    n, d = x1.shape
    tm = min(512, n)
    tok = pl.BlockSpec((tm, d), lambda i: (i, 0))
    const = lambda i: (0, 0)
    return pl.pallas_call(
        functools.partial(_final_norm_kernel, alpha=alpha),
        grid=(n // tm,),
        in_specs=[tok, tok, _resident((1, d), const), _resident((1, d), const)],
        out_specs=tok,
        out_shape=jax.ShapeDtypeStruct((n, d), F32),
        compiler_params=_params("parallel"),
        name="final_norm",
    )(x1, peer_out, ln_g, ln_b)


def _prepare_weights(w):
    (w_in, b_in, lambda_q1, lambda_k1, lambda_q2, lambda_k2, subln_g, w_attn_branch,
     conv_w, conv_b, w_rg_a, b_rg_a, w_rg_x, b_rg_x, rg_lambda, w_rec_branch, w_out,
     ln1_g, ln1_b, w_query, sub_keys1, sub_keys2, peer_u, peer_v, ln2_g, ln2_b) = w
    d = w_in.shape[0]
    hd = lambda_q1.shape[0]
    vd = subln_g.shape[0]
    v_w = w_attn_branch.shape[0]
    n_heads = v_w // vd
    qk_w = n_heads * 2 * hd
    row = lambda a: a.reshape(1, -1).astype(F32)
    w_in_b = w_in.astype(BF16)
    b_in = b_in.astype(F32)
    cuts = (0, qk_w, 2 * qk_w, 2 * qk_w + v_w, w_in.shape[1])
    return dict(
        d=d, hd=hd, vd=vd, n_heads=n_heads,
        w_in=[w_in_b[:, cuts[s]:cuts[s + 1]] for s in range(4)],
        b_in=[b_in[cuts[s]:cuts[s + 1]].reshape(1, -1) for s in range(4)],
        lam_vec=jnp.stack([lambda_q1, lambda_k1, lambda_q2, lambda_k2]).astype(F32),
        subln_g=row(subln_g),
        w_attn=w_attn_branch.astype(BF16), w_rec=w_rec_branch.astype(BF16), w_out=w_out.astype(BF16),
        conv_w=conv_w.astype(F32), conv_b=row(conv_b),
        w_rg=jnp.concatenate([w_rg_a, w_rg_x], axis=-1).astype(BF16),
        b_rg_a=row(b_rg_a), b_rg_x=row(b_rg_x), rg_lambda=row(rg_lambda),
        ln1_g=row(ln1_g), ln1_b=row(ln1_b), ln2_g=row(ln2_g), ln2_b=row(ln2_b),
        wq_t=w_query.T.astype(BF16), sk1=sub_keys1.astype(BF16), sk2=sub_keys2.astype(BF16),
        peer_heads=w_query.shape[1] // (2 * sub_keys1.shape[1]),
        peer_u=peer_u.astype(BF16), peer_vt=peer_v.T.astype(BF16),
    )


def _layer(x, pos, k_past, v_past, conv_buf, h0, lam_init, alpha, p):
    b, t, d = x.shape
    n = b * t
    hd, vd, n_heads = p["hd"], p["vd"], p["n_heads"]
    x2d = x.reshape(n, d)
    assert hd == LANES, "the rotary epilogue maps one q/k half onto one 128-lane tile"

    rope = _rope_tables(pos, b, hd)
    (q_b,) = _project(x2d, p["w_in"][0], p["b_in"][0], rope=rope, emit_f32=False, emit_bf16=True)
    k_f, k_b = _project(x2d, p["w_in"][1], p["b_in"][1], rope=rope, emit_bf16=True)
    v_f, v_b = _project(x2d, p["w_in"][2], p["b_in"][2], emit_bf16=True)
    (zrest,) = _project(x2d, p["w_in"][3], p["b_in"][3])

    q3, k3, v3 = (a.reshape(b, t, -1) for a in (q_b, k_b, v_b))
    if k_past is None:
        oa = _attn_prompt(q3, k3, v3, p["lam_vec"], p["subln_g"], lam_init, n_heads, hd)
    else:
        oa = _attn_sample(q3, k3, v3, k_past.reshape(b, k_past.shape[1], -1),
                          v_past.reshape(b, v_past.shape[1], -1),
                          p["lam_vec"], p["subln_g"], lam_init, n_heads, hd)

    yb, conv_new, h_last = _recurrent(zrest, b, t, d, conv_buf, h0, p["conv_w"], p["conv_b"], p["w_rg"],
                                      p["b_rg_a"], p["b_rg_x"], p["rg_lambda"])

    x1 = _merge(oa.reshape(n, -1), yb.reshape(n, d), zrest, x2d, p["w_attn"], p["w_rec"], p["w_out"],
                p["ln1_g"], p["ln1_b"], alpha)

    x1t = x1.T.astype(BF16)
    e1, e2, pt = _peer_route(x1t, p["wq_t"], p["sk1"], p["sk2"], p["peer_heads"])
    peer_t = _peer_dense(x1t, p["peer_u"], p["peer_vt"], e1, e2, pt)
    y = _final_norm(x1, peer_t.T, p["ln2_g"], p["ln2_b"], alpha)

    return (y.reshape(b, t, d), k_f.reshape(b, t, n_heads, 2, hd), v_f.reshape(b, t, n_heads, vd),
            conv_new, h_last)


def kernel(x_prompt, x_sample, cache_k, cache_v, state_conv, state_h, w_in, b_in, lambda_q1, lambda_k1, lambda_q2, lambda_k2, subln_g, w_attn_branch, conv_w, conv_b, w_rg_a, b_rg_a, w_rg_x, b_rg_x, rg_lambda, w_rec_branch, w_out, ln1_g, ln1_b, w_query, sub_keys1, sub_keys2, peer_u, peer_v, ln2_g, ln2_b):
    layer_weights = (w_in, b_in, lambda_q1, lambda_k1, lambda_q2, lambda_k2, subln_g,
                     w_attn_branch, conv_w, conv_b, w_rg_a, b_rg_a, w_rg_x, b_rg_x, rg_lambda,
                     w_rec_branch, w_out, ln1_g, ln1_b, w_query, sub_keys1, sub_keys2,
                     peer_u, peer_v, ln2_g, ln2_b)
    depth = w_in.shape[0]
    alpha = (2.0 * depth) ** 0.25
    past_len = cache_k.shape[2]
    d_rnn = conv_w.shape[2]
    pos_p = jnp.arange(x_prompt.shape[1])
    pos_s = past_len + jnp.arange(x_sample.shape[1])
    conv0 = jnp.zeros((x_prompt.shape[0], conv_w.shape[1] - 1, d_rnn), x_prompt.dtype)
    h_zero = jnp.zeros((x_prompt.shape[0], d_rnn), x_prompt.dtype)
    yp, ys = x_prompt, x_sample
    outs_p, outs_s = [], []
    for l in range(depth):
        lam_init = 0.8 - 0.6 * math.exp(-0.3 * l)
        p = _prepare_weights(tuple(wt[l] for wt in layer_weights))
        yp, *rest_p = _layer(yp, pos_p, None, None, conv0, h_zero, lam_init, alpha, p)
        ys, *rest_s = _layer(ys, pos_s, cache_k[l], cache_v[l], state_conv[l], state_h[l],
                             lam_init, alpha, p)
        outs_p.append(rest_p)
        outs_s.append(rest_s)
    stack = lambda outs, idx: jnp.stack([o[idx] for o in outs])
    return (yp, ys, stack(outs_p, 0), stack(outs_p, 1), stack(outs_p, 2), stack(outs_p, 3),
            stack(outs_s, 0), stack(outs_s, 1), stack(outs_s, 2), stack(outs_s, 3))
```

```python
import functools
import math

import jax
import jax.numpy as jnp
from jax import lax
from jax.experimental import pallas as pl
from jax.experimental.pallas import tpu as pltpu

F32 = jnp.float32
BF16 = jnp.bfloat16

CHUNK = 64
ROPE_THETA = 500000.0
RG_C = 8.0
TOPK = 16
LN_EPS = 1e-5
NEG_INF = -1e30
LOG2E = 1.4426950408889634

LANES = 128
VMEM_LIMIT_BYTES = 56 * 1024 * 1024


def _params(*sem):
    return pltpu.CompilerParams(dimension_semantics=sem, vmem_limit_bytes=VMEM_LIMIT_BYTES)


def _resident(shape, index_map):
    return pl.BlockSpec(shape, index_map, pipeline_mode=pl.Buffered(1))


def _dot(a, b):
    return jnp.dot(a, b, preferred_element_type=F32)


def _dot_nt(a, b):
    return lax.dot_general(a, b, (((1,), (1,)), ((), ())), preferred_element_type=F32)


def _proj_kernel(*refs, rope_shift, out_scale, emit_f32, emit_bf16):
    it = iter(refs)
    x_ref, w_ref, b_ref = next(it), next(it), next(it)
    if rope_shift:
        c_ref, sa_ref, sb_ref = next(it), next(it), next(it)
    of_ref = next(it) if emit_f32 else None
    ob_ref = next(it) if emit_bf16 else None
    xb_ref = next(it)

    @pl.when(pl.program_id(1) == 0)
    def _():
        xb_ref[...] = x_ref[...].astype(BF16)

    z = _dot(xb_ref[...], w_ref[...]) + b_ref[...]
    if rope_shift:
        c, sa, sb = c_ref[...], sa_ref[...], sb_ref[...]
        parts = []
        for s in range(z.shape[1] // LANES):
            t = z[:, s * LANES:(s + 1) * LANES]
            parts.append(t * c + pltpu.roll(t, rope_shift, 1) * sa
                         + pltpu.roll(t, LANES - rope_shift, 1) * sb)
        z = jnp.concatenate(parts, axis=1) if len(parts) > 1 else parts[0]
    if out_scale != 1.0:
        z = z * out_scale
    if emit_f32:
        of_ref[...] = z
    if emit_bf16:
        ob_ref[...] = z.astype(BF16)


def _project(x2d, w, b, *, rope=None, out_scale=1.0, emit_f32=True, emit_bf16=False):
    n, kdim = x2d.shape
    nout = w.shape[1]
    tm = min(1024, n)
    tn = min(512, nout)
    in_specs = [pl.BlockSpec((tm, kdim), lambda i, j: (i, 0)),
                pl.BlockSpec((kdim, tn), lambda i, j: (0, j)),
                pl.BlockSpec((1, tn), lambda i, j: (0, j))]
    args = [x2d, w, b]
    rope_shift = 0
    if rope is not None:
        rope_shift, tabs = rope
        in_specs += [pl.BlockSpec((tm, LANES), lambda i, j: (i, 0))] * 3
        args += list(tabs)
    out_shape, out_specs = [], []
    if emit_f32:
        out_shape.append(jax.ShapeDtypeStruct((n, nout), F32))
        out_specs.append(pl.BlockSpec((tm, tn), lambda i, j: (i, j)))
    if emit_bf16:
        out_shape.append(jax.ShapeDtypeStruct((n, nout), BF16))
        out_specs.append(pl.BlockSpec((tm, tn), lambda i, j: (i, j)))
    return pl.pallas_call(
        functools.partial(_proj_kernel, rope_shift=rope_shift, out_scale=out_scale, emit_f32=emit_f32,
                          emit_bf16=emit_bf16),
        grid=(n // tm, nout // tn),
        in_specs=in_specs, out_specs=out_specs, out_shape=out_shape,
        scratch_shapes=[pltpu.VMEM((tm, kdim), BF16)],
        compiler_params=_params("parallel", "arbitrary"),
        name="in_proj",
    )(*args)


def _rope_tables(pos, batch, head_dim):
    rope_dim = head_dim // 4
    half = rope_dim // 2
    inv = ROPE_THETA ** (-jnp.arange(half, dtype=F32) / half)
    ang = pos.astype(F32)[:, None] * inv[None, :]
    cos, sin = jnp.cos(ang), jnp.sin(ang)
    t = pos.shape[0]
    ones = jnp.ones((t, head_dim - rope_dim), F32)
    zeros_h = jnp.zeros((t, half), F32)
    zeros_r = jnp.zeros((t, head_dim - rope_dim), F32)
    c = jnp.concatenate([cos, cos, ones], axis=1)
    sa = jnp.concatenate([zeros_h, sin, zeros_r], axis=1)
    sb = jnp.concatenate([-sin, zeros_h, zeros_r], axis=1)
    return half, tuple(jnp.tile(a, (batch, 1)) for a in (c, sa, sb))


def _lambda_value(lam_ref, lam_init):
    lv = lam_ref[...]
    s1 = jnp.sum(lv[0:1] * lv[1:2], axis=1, keepdims=True)
    s2 = jnp.sum(lv[2:3] * lv[3:4], axis=1, keepdims=True)
    return jnp.exp(s1) - jnp.exp(s2) + lam_init


def _sub_norm(o, g, lam_init):
    ms = jnp.mean(o * o, axis=1, keepdims=True)
    return o * lax.rsqrt(ms + LN_EPS) * g * (1.0 - lam_init)


def _attn_prompt_kernel(lam_ref, g_ref, q_ref, k_ref, v_ref, o_ref, s0_ref, s1_ref, p0_ref, p1_ref,
                        a0_ref, a1_ref, acc_ref, m_ref, l_ref, *, tq, hd, lam_init, sub):
    i = pl.program_id(2)
    vd = v_ref.shape[2]
    n_sub = tq // sub
    lane_tiles = tq // LANES
    s_refs, p_refs, a_refs = (s0_ref, s1_ref), (p0_ref, p1_ref), (a0_ref, a1_ref)

    def scores(j, par):
        off = pl.multiple_of(j * tq, tq)
        for comp in range(2):
            s_refs[par][comp] = _dot_nt(q_ref[0, :, comp * hd:(comp + 1) * hd],
                                        k_ref[0, pl.ds(off, tq), comp * hd:(comp + 1) * hd])

    def softmax(par, masked):
        for comp in range(2):
            for r in range(n_sub):
                rows = slice(r * sub, (r + 1) * sub)
                s = s_refs[par][comp, rows, :]
                if masked:
                    q_chunk = (lax.broadcasted_iota(jnp.int32, (sub, tq), 0) + r * sub) // CHUNK
                    k_chunk = lax.broadcasted_iota(jnp.int32, (sub, tq), 1) // CHUNK
                    s = jnp.where(k_chunk <= q_chunk, s, NEG_INF)
                m_prev = m_ref[comp, rows, :]
                m_next = jnp.maximum(m_prev, jnp.max(s, axis=1, keepdims=True))
                alpha = jnp.exp2(m_prev - m_next)
                p = jnp.exp2(s - jnp.tile(m_next, (1, lane_tiles)))
                psum = p[:, 0:LANES]
                for t in range(1, lane_tiles):
                    psum = psum + p[:, t * LANES:(t + 1) * LANES]
                l_ref[comp, rows, :] = alpha * l_ref[comp, rows, :] + psum
                m_ref[comp, rows, :] = m_next
                a_refs[par][comp, rows, :] = alpha
                p_refs[par][comp, rows, :] = p.astype(BF16)

    def accumulate(j, par):
        off = pl.multiple_of(j * tq, tq)
        v = v_ref[0, pl.ds(off, tq), :]
        for comp in range(2):
            acc_ref[comp] = (acc_ref[comp] * jnp.tile(a_refs[par][comp], (1, vd // LANES))
                             + _dot(p_refs[par][comp], v))

    def pipelined_step(j, par):
        accumulate(j - 1, 1 - par)
        softmax(par, False)
        scores(j + 1, 1 - par)

    acc_ref[...] = jnp.zeros_like(acc_ref)
    m_ref[...] = jnp.full_like(m_ref, NEG_INF)
    l_ref[...] = jnp.zeros_like(l_ref)

    scores(0, 0)

    @pl.when(i > 0)
    def _():
        softmax(0, False)
        scores(1, 1)

    def pair(t, carry):
        pipelined_step(2 * t + 1, 1)
        pipelined_step(2 * t + 2, 0)
        return carry

    lax.fori_loop(0, (i - 1) // 2, pair, 0)

    @pl.when(jnp.logical_and(i > 1, i % 2 == 0))
    def _():
        pipelined_step(i - 1, 1)

    @pl.when(i % 2 == 0)
    def _():
        @pl.when(i > 0)
        def _():
            accumulate(i - 1, 1)
        softmax(0, True)
        accumulate(i, 0)

    @pl.when(i % 2 == 1)
    def _():
        accumulate(i - 1, 0)
        softmax(1, True)
        accumulate(i, 1)

    lam = _lambda_value(lam_ref, lam_init)
    l1 = jnp.sum(l_ref[0], axis=1, keepdims=True)
    l2 = jnp.sum(l_ref[1], axis=1, keepdims=True)
    o = acc_ref[0] / l1 - lam * (acc_ref[1] / l2)
    o_ref[0] = _sub_norm(o, g_ref[...], lam_init).astype(o_ref.dtype)


def _attn_prompt(q, k, v, lam_vec, g, lam_init, n_heads, hd):
    b, t, _ = q.shape
    vd = v.shape[2] // n_heads
    tq = min(512, t)
    sub = min(64, tq)
    kern = functools.partial(_attn_prompt_kernel, tq=tq, hd=hd, lam_init=lam_init, sub=sub)
    two = lambda shape, dt: [pltpu.VMEM(shape, dt), pltpu.VMEM(shape, dt)]
    return pl.pallas_call(
        kern,
        grid=(b, n_heads, t // tq),
        in_specs=[_resident((4, hd), lambda bi, h, i: (0, 0)),
                  _resident((1, vd), lambda bi, h, i: (0, 0)),
                  pl.BlockSpec((1, tq, 2 * hd), lambda bi, h, i: (bi, i, h)),
                  pl.BlockSpec((1, t, 2 * hd), lambda bi, h, i: (bi, 0, h)),
                  pl.BlockSpec((1, t, vd), lambda bi, h, i: (bi, 0, h))],
        out_specs=pl.BlockSpec((1, tq, vd), lambda bi, h, i: (bi, i, h)),
        out_shape=jax.ShapeDtypeStruct((b, t, n_heads * vd), BF16),
        scratch_shapes=(two((2, tq, tq), F32) + two((2, tq, tq), BF16) + two((2, tq, LANES), F32)
                        + [pltpu.VMEM((2, tq, vd), F32), pltpu.VMEM((2, tq, LANES), F32),
                           pltpu.VMEM((2, tq, LANES), F32)]),
        compiler_params=_params("parallel", "parallel", "arbitrary"),
        name="attn_prompt",
    )(lam_vec, g, q, k, v)


def _attn_sample_kernel(lam_ref, g_ref, q_ref, kn_ref, vn_ref, kp_ref, vp_ref, o_ref,
                        *, hd, lam_init):
    q, kn, vn = q_ref[0], kn_ref[0], vn_ref[0]
    kp = kp_ref[0].astype(BF16)
    vp = vp_ref[0].astype(BF16)
    outs = []
    for comp in range(2):
        lo, hi = comp * hd, (comp + 1) * hd
        sp = _dot_nt(q[:, lo:hi], kp[:, lo:hi])
        sn = _dot_nt(q[:, lo:hi], kn[:, lo:hi])
        m = jnp.maximum(jnp.max(sp, axis=1, keepdims=True), jnp.max(sn, axis=1, keepdims=True))
        pp = jnp.exp2(sp - m)
        pn = jnp.exp2(sn - m)
        l = jnp.sum(pp, axis=1, keepdims=True) + jnp.sum(pn, axis=1, keepdims=True)
        outs.append((_dot(pp.astype(BF16), vp) + _dot(pn.astype(BF16), vn)) / l)
    lam = _lambda_value(lam_ref, lam_init)
    o = outs[0] - lam * outs[1]
    o_ref[0] = _sub_norm(o, g_ref[...], lam_init).astype(o_ref.dtype)


def _attn_sample(q, k, v, k_past, v_past, lam_vec, g, lam_init, n_heads, hd):
    b, t, _ = q.shape
    p = k_past.shape[1]
    vd = v.shape[2] // n_heads
    kern = functools.partial(_attn_sample_kernel, hd=hd, lam_init=lam_init)
    return pl.pallas_call(
        kern,
        grid=(b, n_heads),
        in_specs=[_resident((4, hd), lambda bi, h: (0, 0)),
                  _resident((1, vd), lambda bi, h: (0, 0)),
                  pl.BlockSpec((1, t, 2 * hd), lambda bi, h: (bi, 0, h)),
                  pl.BlockSpec((1, t, 2 * hd), lambda bi, h: (bi, 0, h)),
                  pl.BlockSpec((1, t, vd), lambda bi, h: (bi, 0, h)),
                  pl.BlockSpec((1, p, 2 * hd), lambda bi, h: (bi, 0, h)),
                  pl.BlockSpec((1, p, vd), lambda bi, h: (bi, 0, h))],
        out_specs=pl.BlockSpec((1, t, vd), lambda bi, h: (bi, 0, h)),
        out_shape=jax.ShapeDtypeStruct((b, t, n_heads * vd), BF16),
        compiler_params=_params("parallel", "parallel"),
        name="attn_sample",
    )(lam_vec, g, q, k, v, k_past, v_past)


def _recurrent_kernel(xr_ref, xg_ref, cbuf_ref, h0_ref, cw_ref, cb_ref, wrg_ref, bra_ref, brx_ref,
                      lam_ref, y_ref, cnew_ref, hlast_ref, xpad_ref, a_ref, u_ref, hs_ref, hc_ref,
                      *, tt, cw, bw):
    i = pl.program_id(1)
    hdr = 8
    d = xr_ref.shape[2]

    @pl.when(i == 0)
    def _():
        xpad_ref[hdr - (cw - 1):hdr, :] = cbuf_ref[0]
        hc_ref[...] = h0_ref[0]

    xpad_ref[hdr:hdr + tt, :] = xr_ref[0]
    xc = cb_ref[...] + xpad_ref[hdr - (cw - 1):hdr - (cw - 1) + tt, :] * cw_ref[0:1, :]
    for j in range(1, cw):
        lo = hdr - (cw - 1) + j
        xc = xc + xpad_ref[lo:lo + tt, :] * cw_ref[j:j + 1, :]
    tail = xpad_ref[hdr + tt - (cw - 1):hdr + tt, :]
    cnew_ref[0] = tail
    xpad_ref[hdr - (cw - 1):hdr, :] = tail

    for n in range(d // bw):
        lo, hi = n * bw, (n + 1) * bw
        xcn = xc[:, lo:hi]
        gates = _dot(xcn.astype(BF16), wrg_ref[n])
        r = jax.nn.sigmoid(gates[:, :bw] + bra_ref[:, lo:hi])
        gi = jax.nn.sigmoid(gates[:, bw:] + brx_ref[:, lo:hi])
        log_a = -RG_C * r * jax.nn.softplus(-lam_ref[:, lo:hi])
        a = jnp.exp(log_a)
        a_ref[:, lo:hi] = a
        u_ref[:, lo:hi] = jnp.sqrt(-jnp.tanh(log_a) * (a * a + 1.0)) * (gi * xcn)

    def scan_row(t, h):
        h = a_ref[pl.ds(t, 1), :] * h + u_ref[pl.ds(t, 1), :]
        hs_ref[pl.ds(t, 1), :] = h
        return h

    h = lax.fori_loop(0, tt, scan_row, hc_ref[...])
    hc_ref[...] = h
    hlast_ref[0] = h
    y_ref[0] = (hs_ref[...] * jax.nn.gelu(xg_ref[0])).astype(y_ref.dtype)


def _recurrent(zrest, b, t, d, conv_buf, h0, conv_w, conv_b, w_rg, b_rg_a, b_rg_x, rg_lambda):
    cw = conv_w.shape[0]
    bw = w_rg.shape[1]
    tt = min(256, t)
    z3 = zrest.reshape(b, t, zrest.shape[1])
    kern = functools.partial(_recurrent_kernel, tt=tt, cw=cw, bw=bw)
    row = lambda bi, i: (0, 0)
    y, cnew, hlast = pl.pallas_call(
        kern,
        grid=(b, t // tt),
        in_specs=[pl.BlockSpec((1, tt, d), lambda bi, i: (bi, i, 0)),
                  pl.BlockSpec((1, tt, d), lambda bi, i: (bi, i, 1)),
                  pl.BlockSpec((1, cw - 1, d), lambda bi, i: (bi, 0, 0)),
                  pl.BlockSpec((1, 1, d), lambda bi, i: (bi, 0, 0)),
                  _resident((cw, d), row),
                  _resident((1, d), row),
                  _resident(w_rg.shape, lambda bi, i: (0, 0, 0)),
                  _resident((1, d), row),
                  _resident((1, d), row),
                  _resident((1, d), row)],
        out_specs=[pl.BlockSpec((1, tt, d), lambda bi, i: (bi, i, 0)),
                   pl.BlockSpec((1, cw - 1, d), lambda bi, i: (bi, 0, 0)),
                   pl.BlockSpec((1, 1, d), lambda bi, i: (bi, 0, 0))],
        out_shape=[jax.ShapeDtypeStruct((b, t, d), BF16),
                   jax.ShapeDtypeStruct((b, cw - 1, d), F32),
                   jax.ShapeDtypeStruct((b, 1, d), F32)],
        scratch_shapes=[pltpu.VMEM((tt + 8, d), F32), pltpu.VMEM((tt, d), F32),
                        pltpu.VMEM((tt, d), F32), pltpu.VMEM((tt, d), F32), pltpu.VMEM((1, d), F32)],
        compiler_params=_params("parallel", "arbitrary"),
        name="recurrent",
    )(z3, z3, conv_buf, h0.reshape(b, 1, d), conv_w, conv_b, w_rg, b_rg_a, b_rg_x, rg_lambda)
    return y, cnew, hlast.reshape(b, d)


def _layer_norm(x, g, b):
    mu = jnp.mean(x, axis=1, keepdims=True)
    xc = x - mu
    var = jnp.mean(xc * xc, axis=1, keepdims=True)
    return xc * lax.rsqrt(var + LN_EPS) * g + b


def _merge_kernel(oa_ref, yb_ref, ga_ref, gr_ref, x_ref, wa_ref, wb_ref, wo_ref, g_ref, b_ref,
                  x1_ref, x1t_ref, *, alpha):
    br_a = _dot(oa_ref[...], wa_ref[...])
    br_b = _dot(yb_ref[...], wb_ref[...])
    m = jax.nn.sigmoid(ga_ref[...]) * br_a + jax.nn.sigmoid(gr_ref[...]) * br_b
    r = _dot(m.astype(BF16), wo_ref[...])
    x1 = _layer_norm(alpha * x_ref[...] + r, g_ref[...], b_ref[...])
    x1_ref[...] = x1
    x1t_ref[...] = x1.T.astype(BF16)


def _merge(oa, yb, zrest, x2d, w_attn, w_rec, w_out, ln_g, ln_b, alpha):
    n, d = x2d.shape
    tm = min(256, n)
    tok = lambda c: pl.BlockSpec((tm, d), lambda i: (i, c))
    const = lambda i: (0, 0)
    return pl.pallas_call(
        functools.partial(_merge_kernel, alpha=alpha),
        grid=(n // tm,),
        in_specs=[pl.BlockSpec((tm, oa.shape[1]), lambda i: (i, 0)), tok(0), tok(2), tok(3), tok(0),
                  _resident(w_attn.shape, const), _resident(w_rec.shape, const),
                  _resident(w_out.shape, const), _resident((1, d), const), _resident((1, d), const)],
        out_specs=[tok(0), pl.BlockSpec((d, tm), lambda i: (0, i))],
        out_shape=[jax.ShapeDtypeStruct((n, d), F32), jax.ShapeDtypeStruct((d, n), BF16)],
        compiler_params=_params("parallel"),
        name="merge",
    )(oa, yb, zrest, zrest, x2d, w_attn, w_rec, w_out, ln_g, ln_b)


def _top_rows(x, k, fill):
    rows = []
    for it in range(k):
        m = jnp.max(x, axis=0, keepdims=True)
        rows.append(m)
        if it + 1 < k:
            x = jnp.where(x == m, fill, x)
    return rows


def _candidate_pairs(k):
    return [(a, b) for a in range(k) for b in range(k) if (a + 1) * (b + 1) <= k]


def _peer_route_kernel(xt_ref, wq_ref, sk1_ref, sk2_ref, e1_ref, e2_ref, pt_ref, *, heads, half):
    qt = _dot(wq_ref[...], xt_ref[...]).astype(BF16)
    tm = qt.shape[1]
    pairs = _candidate_pairs(TOPK)
    pad_rows = (-len(pairs)) % 8
    pad = [jnp.full((pad_rows, tm), -1.0, F32)] if pad_rows else []

    def pair_products(ea, eb):
        return jnp.concatenate([ea[a] * eb[b] for a, b in pairs] + pad, axis=0)

    for h in range(heads):
        base = h * 2 * half
        s1 = _dot(sk1_ref[...], qt[base:base + half])
        s2 = _dot(sk2_ref[...], qt[base + half:base + 2 * half])
        v1 = _top_rows(s1, TOPK, -jnp.inf)
        v2 = _top_rows(s2, TOPK, -jnp.inf)
        e1 = jnp.exp(s1 - v1[0])
        e2 = jnp.exp(s2 - v2[0])
        e1_top = [jnp.exp(v - v1[0]) for v in v1]
        e2_top = [jnp.exp(v - v2[0]) for v in v2]
        tops = _top_rows(pair_products(e1_top, e2_top), TOPK, -1.0)
        z = functools.reduce(lambda acc, r: acc + jnp.maximum(r, 0.0), tops[1:], tops[0])
        rz = 1.0 / z
        e1n = e1 * rz
        e1n_top = [e * rz for e in e1_top]
        thr = _top_rows(pair_products(e1n_top, e2_top), TOPK, -1.0)[-1]
        e1_ref[h] = e1n
        e2_ref[h] = e2
        pt_ref[h:h + 1, :] = jnp.maximum(thr, 0.0)


def _peer_route(x1t, wq_t, sk1, sk2, heads):
    d, n = x1t.shape
    nk, half = sk1.shape
    tm = min(512, n)
    const = lambda i: (0, 0)
    return pl.pallas_call(
        functools.partial(_peer_route_kernel, heads=heads, half=half),
        grid=(n // tm,),
        in_specs=[pl.BlockSpec((d, tm), lambda i: (0, i)),
                  _resident(wq_t.shape, const), _resident(sk1.shape, const), _resident(sk2.shape, const)],
        out_specs=[pl.BlockSpec((heads, nk, tm), lambda i: (0, 0, i)),
                   pl.BlockSpec((heads, nk, tm), lambda i: (0, 0, i)),
                   pl.BlockSpec((heads, tm), lambda i: (0, i))],
        out_shape=[jax.ShapeDtypeStruct((heads, nk, n), F32),
                   jax.ShapeDtypeStruct((heads, nk, n), F32),
                   jax.ShapeDtypeStruct((heads, n), F32)],
        compiler_params=_params("parallel"),
        name="peer_route",
    )(x1t, wq_t, sk1, sk2)


def _peer_dense_kernel(xt_ref, u_ref, vt_ref, e1_ref, e2_ref, pt_ref, x1_ref, g_ref, b_ref, y_ref, acc_ref,
                       *, heads, nk, rows_per_step, alpha):
    j = pl.program_id(1)

    @pl.when(j == 0)
    def _():
        acc_ref[...] = jnp.zeros_like(acc_ref)

    act = jax.nn.gelu(_dot(u_ref[...], xt_ref[...]))
    parts = []
    for a in range(rows_per_step):
        w = None
        for h in range(heads):
            p = e1_ref[h, a:a + 1, :] * e2_ref[h]
            sel = jnp.where(p >= pt_ref[h:h + 1, :], p, 0.0)
            w = sel if w is None else w + sel
        parts.append((w * act[a * nk:(a + 1) * nk]).astype(BF16))
    wa = jnp.concatenate(parts, axis=0) if len(parts) > 1 else parts[0]
    acc_ref[...] += _dot(vt_ref[...], wa)

    @pl.when(j == pl.num_programs(1) - 1)
    def _():
        y_ref[...] = _layer_norm(alpha * x1_ref[...] + acc_ref[...].T, g_ref[...], b_ref[...])


def _peer_dense(x1, x1t, u, vt, e1, e2, pt, ln_g, ln_b, alpha):
    d, n = x1t.shape
    heads, nk, _ = e1.shape
    ne = u.shape[0]
    tm = min(512, n)
    rows_per_step = 8
    ec = rows_per_step * nk
    kern = functools.partial(_peer_dense_kernel, heads=heads, nk=nk, rows_per_step=rows_per_step, alpha=alpha)
    return pl.pallas_call(
        kern,
        grid=(n // tm, ne // ec),
        in_specs=[_resident((d, tm), lambda i, j: (0, i)),
                  pl.BlockSpec((ec, d), lambda i, j: (j, 0)),
                  pl.BlockSpec((d, ec), lambda i, j: (0, j)),
                  pl.BlockSpec((heads, rows_per_step, tm), lambda i, j: (0, j, i)),
                  _resident((heads, nk, tm), lambda i, j: (0, 0, i)),
                  _resident((heads, tm), lambda i, j: (0, i)),
                  _resident((tm, d), lambda i, j: (i, 0)),
                  _resident((1, d), lambda i, j: (0, 0)),
                  _resident((1, d), lambda i, j: (0, 0))],
        out_specs=pl.BlockSpec((tm, d), lambda i, j: (i, 0)),
        out_shape=jax.ShapeDtypeStruct((n, d), F32),
        scratch_shapes=[pltpu.VMEM((d, tm), F32)],
        compiler_params=_params("parallel", "arbitrary"),
        name="peer_dense",
    )(x1t, u, vt, e1, e2, pt, x1, ln_g, ln_b)


def _cast_kernel(w_ref, o_ref, *, transpose):
    w = w_ref[...]
    o_ref[...] = (w.T if transpose else w).astype(o_ref.dtype)


def _cast_table(w3d, layer, *, transpose):
    _, rows, cols = w3d.shape
    tr, tc = min(512, rows), min(512, cols)
    out_shape = (cols, rows) if transpose else (rows, cols)
    out_block = (tc, tr) if transpose else (tr, tc)
    out_map = (lambda i, j: (j, i)) if transpose else (lambda i, j: (i, j))
    return pl.pallas_call(
        functools.partial(_cast_kernel, transpose=transpose),
        grid=(rows // tr, cols // tc),
        in_specs=[pl.BlockSpec((None, tr, tc), lambda i, j: (layer, i, j))],
        out_specs=pl.BlockSpec(out_block, out_map),
        out_shape=jax.ShapeDtypeStruct(out_shape, BF16),
        compiler_params=_params("parallel", "parallel"),
        name="cast_table",
    )(w3d)


def _prepare_weights(w):
    (w_in, b_in, lambda_q1, lambda_k1, lambda_q2, lambda_k2, subln_g, w_attn_branch,
     conv_w, conv_b, w_rg_a, b_rg_a, w_rg_x, b_rg_x, rg_lambda, w_rec_branch, w_out,
     ln1_g, ln1_b, w_query, sub_keys1, sub_keys2, peer_u, peer_v, ln2_g, ln2_b) = w
    d = w_in.shape[0]
    hd = lambda_q1.shape[0]
    vd = subln_g.shape[0]
    v_w = w_attn_branch.shape[0]
    n_heads = v_w // vd
    qk_w = n_heads * 2 * hd
    row = lambda a: a.reshape(1, -1).astype(F32)
    w_in_b = w_in.astype(BF16)
    b_in = b_in.astype(F32)
    cuts = (0, qk_w, 2 * qk_w, 2 * qk_w + v_w, w_in.shape[1])
    return dict(
        d=d, hd=hd, vd=vd, n_heads=n_heads,
        w_in=[w_in_b[:, cuts[s]:cuts[s + 1]] for s in range(4)],
        b_in=[b_in[cuts[s]:cuts[s + 1]].reshape(1, -1) for s in range(4)],
        lam_vec=jnp.stack([lambda_q1, lambda_k1, lambda_q2, lambda_k2]).astype(F32),
        subln_g=row(subln_g),
        w_attn=w_attn_branch.astype(BF16), w_rec=w_rec_branch.astype(BF16), w_out=w_out.astype(BF16),
        conv_w=conv_w.astype(F32), conv_b=row(conv_b),
        w_rg=jnp.concatenate([w_rg_a, w_rg_x], axis=-1).astype(BF16),
        b_rg_a=row(b_rg_a), b_rg_x=row(b_rg_x), rg_lambda=row(rg_lambda),
        ln1_g=row(ln1_g), ln1_b=row(ln1_b), ln2_g=row(ln2_g), ln2_b=row(ln2_b),
        wq_t=w_query.T.astype(BF16), sk1=sub_keys1.astype(BF16), sk2=sub_keys2.astype(BF16),
        peer_heads=w_query.shape[1] // (2 * sub_keys1.shape[1]),
    )


def _layer(x, pos, k_past, v_past, conv_buf, h0, lam_init, alpha, p):
    b, t, d = x.shape
    n = b * t
    hd, vd, n_heads = p["hd"], p["vd"], p["n_heads"]
    x2d = x.reshape(n, d)
    assert hd == LANES, "the rotary epilogue maps one q/k half onto one 128-lane tile"

    rope = _rope_tables(pos, b, hd)
    (q_b,) = _project(x2d, p["w_in"][0], p["b_in"][0], rope=rope, out_scale=hd ** -0.5 * LOG2E,
                      emit_f32=False, emit_bf16=True)
    k_f, k_b = _project(x2d, p["w_in"][1], p["b_in"][1], rope=rope, emit_bf16=True)
    v_f, v_b = _project(x2d, p["w_in"][2], p["b_in"][2], emit_bf16=True)
    (zrest,) = _project(x2d, p["w_in"][3], p["b_in"][3])

    q3, k3, v3 = (a.reshape(b, t, -1) for a in (q_b, k_b, v_b))
    if k_past is None:
        oa = _attn_prompt(q3, k3, v3, p["lam_vec"], p["subln_g"], lam_init, n_heads, hd)
    else:
        oa = _attn_sample(q3, k3, v3, k_past.reshape(b, k_past.shape[1], -1),
                          v_past.reshape(b, v_past.shape[1], -1),
                          p["lam_vec"], p["subln_g"], lam_init, n_heads, hd)

    yb, conv_new, h_last = _recurrent(zrest, b, t, d, conv_buf, h0, p["conv_w"], p["conv_b"], p["w_rg"],
                                      p["b_rg_a"], p["b_rg_x"], p["rg_lambda"])

    x1, x1t = _merge(oa.reshape(n, -1), yb.reshape(n, d), zrest, x2d, p["w_attn"], p["w_rec"], p["w_out"],
                     p["ln1_g"], p["ln1_b"], alpha)

    e1, e2, pt = _peer_route(x1t, p["wq_t"], p["sk1"], p["sk2"], p["peer_heads"])
    y = _peer_dense(x1, x1t, p["peer_u"], p["peer_vt"], e1, e2, pt, p["ln2_g"], p["ln2_b"], alpha)

    return (y.reshape(b, t, d), k_f.reshape(b, t, n_heads, 2, hd), v_f.reshape(b, t, n_heads, vd),
            conv_new, h_last)


def kernel(x_prompt, x_sample, cache_k, cache_v, state_conv, state_h, w_in, b_in, lambda_q1, lambda_k1, lambda_q2, lambda_k2, subln_g, w_attn_branch, conv_w, conv_b, w_rg_a, b_rg_a, w_rg_x, b_rg_x, rg_lambda, w_rec_branch, w_out, ln1_g, ln1_b, w_query, sub_keys1, sub_keys2, peer_u, peer_v, ln2_g, ln2_b):
    layer_weights = (w_in, b_in, lambda_q1, lambda_k1, lambda_q2, lambda_k2, subln_g,
                     w_attn_branch, conv_w, conv_b, w_rg_a, b_rg_a, w_rg_x, b_rg_x, rg_lambda,
                     w_rec_branch, w_out, ln1_g, ln1_b, w_query, sub_keys1, sub_keys2,
                     peer_u, peer_v, ln2_g, ln2_b)
    depth = w_in.shape[0]
    alpha = (2.0 * depth) ** 0.25
    past_len = cache_k.shape[2]
    d_rnn = conv_w.shape[2]
    pos_p = jnp.arange(x_prompt.shape[1])
    pos_s = past_len + jnp.arange(x_sample.shape[1])
    conv0 = jnp.zeros((x_prompt.shape[0], conv_w.shape[1] - 1, d_rnn), x_prompt.dtype)
    h_zero = jnp.zeros((x_prompt.shape[0], d_rnn), x_prompt.dtype)
    yp, ys = x_prompt, x_sample
    outs_p, outs_s = [], []
    for l in range(depth):
        lam_init = 0.8 - 0.6 * math.exp(-0.3 * l)
        p = _prepare_weights(tuple(wt[l] for wt in layer_weights))
        p["peer_u"] = _cast_table(peer_u, l, transpose=False)
        p["peer_vt"] = _cast_table(peer_v, l, transpose=True)
        yp, *rest_p = _layer(yp, pos_p, None, None, conv0, h_zero, lam_init, alpha, p)
        ys, *rest_s = _layer(ys, pos_s, cache_k[l], cache_v[l], state_conv[l], state_h[l],
                             lam_init, alpha, p)
        outs_p.append(rest_p)
        outs_s.append(rest_s)
    stack = lambda outs, idx: jnp.stack([o[idx] for o in outs])
    return (yp, ys, stack(outs_p, 0), stack(outs_p, 1), stack(outs_p, 2), stack(outs_p, 3),
            stack(outs_s, 0), stack(outs_s, 1), stack(outs_s, 2), stack(outs_s, 3))
```

```python
import functools
import math

import jax
import jax.numpy as jnp
from jax import lax
from jax.experimental import pallas as pl
from jax.experimental.pallas import tpu as pltpu

F32 = jnp.float32
BF16 = jnp.bfloat16

CHUNK = 64
ROPE_THETA = 500000.0
RG_C = 8.0
TOPK = 16
LN_EPS = 1e-5
NEG_INF = -1e30
LOG2E = 1.4426950408889634

LANES = 128
VMEM_LIMIT_BYTES = 56 * 1024 * 1024


def _params(*sem):
    return pltpu.CompilerParams(dimension_semantics=sem, vmem_limit_bytes=VMEM_LIMIT_BYTES)


def _resident(shape, index_map):
    return pl.BlockSpec(shape, index_map, pipeline_mode=pl.Buffered(1))


def _dot(a, b):
    return jnp.dot(a, b, preferred_element_type=F32)


def _dot_nt(a, b):
    return lax.dot_general(a, b, (((1,), (1,)), ((), ())), preferred_element_type=F32)


def _proj_kernel(*refs, rope_shift, out_scale, emit_f32, emit_bf16, emit_xb, split_rows):
    it = iter(refs)
    x_ref, w_ref, b_ref = next(it), next(it), next(it)
    if rope_shift:
        c_ref, sa_ref, sb_ref = next(it), next(it), next(it)
    of_ref = next(it) if emit_f32 else None
    ob_ref = next(it) if emit_bf16 else None
    xb_ref = next(it) if emit_xb else None

    x = x_ref[...].astype(BF16)
    if emit_xb:
        xb_ref[...] = x
    z = _dot(x, w_ref[...]) + b_ref[...]
    n_tiles = z.shape[1] // LANES
    if rope_shift:
        c, sa, sb = c_ref[...], sa_ref[...], sb_ref[...]
        parts = []
        for s in range(n_tiles):
            t = z[:, s * LANES:(s + 1) * LANES]
            parts.append(t * c + pltpu.roll(t, rope_shift, 1) * sa
                         + pltpu.roll(t, LANES - rope_shift, 1) * sb)
        z = jnp.concatenate(parts, axis=1) if len(parts) > 1 else parts[0]
    if out_scale != 1.0:
        z = z * out_scale
    if emit_f32 and split_rows:
        for s in range(n_tiles):
            of_ref[pl.ds(s, z.shape[0], stride=n_tiles), :] = z[:, s * LANES:(s + 1) * LANES]
    elif emit_f32:
        of_ref[...] = z
    if emit_bf16:
        ob_ref[...] = z.astype(BF16)


def _project(x2d, w, b, col0, nout, *, rope=None, out_scale=1.0, emit_f32=True, emit_bf16=False,
             emit_xb=False, split_rows=False, tm=1024, tn=512):
    n, kdim = x2d.shape
    tm, tn = min(tm, n), min(tn, nout)
    assert not (split_rows and tn != nout) and col0 % tn == 0
    jb = col0 // tn
    w_spec = (_resident((kdim, tn), lambda i, j: (0, jb + j)) if tn == nout
              else pl.BlockSpec((kdim, tn), lambda i, j: (0, jb + j)))
    in_specs = [pl.BlockSpec((tm, kdim), lambda i, j: (i, 0)), w_spec,
                pl.BlockSpec((1, tn), lambda i, j: (0, jb + j))]
    args = [x2d, w, b]
    rope_shift = 0
    if rope is not None:
        rope_shift, tabs = rope
        in_specs += [pl.BlockSpec((tm, LANES), lambda i, j: (i, 0))] * 3
        args += list(tabs)
    out_shape, out_specs = [], []
    if emit_f32 and split_rows:
        tiles = nout // LANES
        out_shape.append(jax.ShapeDtypeStruct((n * tiles, LANES), F32))
        out_specs.append(pl.BlockSpec((tm * tiles, LANES), lambda i, j: (i, 0)))
    elif emit_f32:
        out_shape.append(jax.ShapeDtypeStruct((n, nout), F32))
        out_specs.append(pl.BlockSpec((tm, tn), lambda i, j: (i, j)))
    if emit_bf16:
        out_shape.append(jax.ShapeDtypeStruct((n, nout), BF16))
        out_specs.append(pl.BlockSpec((tm, tn), lambda i, j: (i, j)))
    if emit_xb:
        out_shape.append(jax.ShapeDtypeStruct((n, kdim), BF16))
        out_specs.append(pl.BlockSpec((tm, kdim), lambda i, j: (i, 0)))
    return pl.pallas_call(
        functools.partial(_proj_kernel, rope_shift=rope_shift, out_scale=out_scale, emit_f32=emit_f32,
                          emit_bf16=emit_bf16, emit_xb=emit_xb, split_rows=split_rows),
        grid=(n // tm, nout // tn),
        in_specs=in_specs, out_specs=out_specs, out_shape=out_shape,
        compiler_params=_params("parallel", "arbitrary"),
        name="in_proj",
    )(*args)


def _rope_tables(pos, batch, head_dim):
    rope_dim = head_dim // 4
    half = rope_dim // 2
    inv = ROPE_THETA ** (-jnp.arange(half, dtype=F32) / half)
    ang = pos.astype(F32)[:, None] * inv[None, :]
    cos, sin = jnp.cos(ang), jnp.sin(ang)
    t = pos.shape[0]
    ones = jnp.ones((t, head_dim - rope_dim), F32)
    zeros_h = jnp.zeros((t, half), F32)
    zeros_r = jnp.zeros((t, head_dim - rope_dim), F32)
    c = jnp.concatenate([cos, cos, ones], axis=1)
    sa = jnp.concatenate([zeros_h, sin, zeros_r], axis=1)
    sb = jnp.concatenate([-sin, zeros_h, zeros_r], axis=1)
    return half, tuple(jnp.tile(a, (batch, 1)) for a in (c, sa, sb))


def _lambda_value(lam_ref, lam_init):
    lv = lam_ref[...]
    s1 = jnp.sum(lv[0:1] * lv[1:2], axis=1, keepdims=True)
    s2 = jnp.sum(lv[2:3] * lv[3:4], axis=1, keepdims=True)
    return jnp.exp(s1) - jnp.exp(s2) + lam_init


def _sub_norm(o, g, lam_init):
    ms = jnp.mean(o * o, axis=1, keepdims=True)
    return o * lax.rsqrt(ms + LN_EPS) * g * (1.0 - lam_init)


def _attn_prompt_kernel(lam_ref, g_ref, q_ref, k_ref, v_ref, o_ref, s0_ref, s1_ref, p0_ref, p1_ref,
                        a0_ref, a1_ref, acc_ref, m_ref, l_ref, *, tq, hd, lam_init, sub):
    i = pl.program_id(2)
    vd = v_ref.shape[2]
    n_sub = tq // sub
    lane_tiles = tq // LANES
    s_refs, p_refs, a_refs = (s0_ref, s1_ref), (p0_ref, p1_ref), (a0_ref, a1_ref)

    def scores(j, par):
        off = pl.multiple_of(j * tq, tq)
        for comp in range(2):
            s_refs[par][comp] = _dot_nt(q_ref[0, :, comp * hd:(comp + 1) * hd],
                                        k_ref[0, pl.ds(off, tq), comp * hd:(comp + 1) * hd])

    def softmax(par, masked):
        for comp in range(2):
            for r in range(n_sub):
                rows = slice(r * sub, (r + 1) * sub)
                s = s_refs[par][comp, rows, :]
                if masked:
                    q_chunk = (lax.broadcasted_iota(jnp.int32, (sub, tq), 0) + r * sub) // CHUNK
                    k_chunk = lax.broadcasted_iota(jnp.int32, (sub, tq), 1) // CHUNK
                    s = jnp.where(k_chunk <= q_chunk, s, NEG_INF)
                m_prev = m_ref[comp, rows, :]
                m_next = jnp.maximum(m_prev, jnp.max(s, axis=1, keepdims=True))
                alpha = jnp.exp2(m_prev - m_next)
                p = jnp.exp2(s - jnp.tile(m_next, (1, lane_tiles)))
                psum = p[:, 0:LANES]
                for t in range(1, lane_tiles):
                    psum = psum + p[:, t * LANES:(t + 1) * LANES]
                l_ref[comp, rows, :] = alpha * l_ref[comp, rows, :] + psum
                m_ref[comp, rows, :] = m_next
                a_refs[par][comp, rows, :] = alpha
                p_refs[par][comp, rows, :] = p.astype(BF16)

    def accumulate(j, par):
        off = pl.multiple_of(j * tq, tq)
        v = v_ref[0, pl.ds(off, tq), :]
        for comp in range(2):
            acc_ref[comp] = (acc_ref[comp] * jnp.tile(a_refs[par][comp], (1, vd // LANES))
                             + _dot(p_refs[par][comp], v))

    def pipelined_step(j, par):
        accumulate(j - 1, 1 - par)
        softmax(par, False)
        scores(j + 1, 1 - par)

    acc_ref[...] = jnp.zeros_like(acc_ref)
    m_ref[...] = jnp.full_like(m_ref, NEG_INF)
    l_ref[...] = jnp.zeros_like(l_ref)

    scores(0, 0)

    @pl.when(i > 0)
    def _():
        softmax(0, False)
        scores(1, 1)

    def pair(t, carry):
        pipelined_step(2 * t + 1, 1)
        pipelined_step(2 * t + 2, 0)
        return carry

    lax.fori_loop(0, (i - 1) // 2, pair, 0)

    @pl.when(jnp.logical_and(i > 1, i % 2 == 0))
    def _():
        pipelined_step(i - 1, 1)

    @pl.when(i % 2 == 0)
    def _():
        @pl.when(i > 0)
        def _():
            accumulate(i - 1, 1)
        softmax(0, True)
        accumulate(i, 0)

    @pl.when(i % 2 == 1)
    def _():
        accumulate(i - 1, 0)
        softmax(1, True)
        accumulate(i, 1)

    lam = _lambda_value(lam_ref, lam_init)
    l1 = jnp.sum(l_ref[0], axis=1, keepdims=True)
    l2 = jnp.sum(l_ref[1], axis=1, keepdims=True)
    o = acc_ref[0] / l1 - lam * (acc_ref[1] / l2)
    o_ref[0] = _sub_norm(o, g_ref[...], lam_init).astype(o_ref.dtype)


def _attn_prompt(q, k, v, lam_vec, g, lam_init, n_heads, hd):
    b, t, _ = q.shape
    vd = v.shape[2] // n_heads
    tq = min(512, t)
    sub = min(64, tq)
    kern = functools.partial(_attn_prompt_kernel, tq=tq, hd=hd, lam_init=lam_init, sub=sub)
    two = lambda shape, dt: [pltpu.VMEM(shape, dt), pltpu.VMEM(shape, dt)]
    return pl.pallas_call(
        kern,
        grid=(b, n_heads, t // tq),
        in_specs=[_resident((4, hd), lambda bi, h, i: (0, 0)),
                  _resident((1, vd), lambda bi, h, i: (0, 0)),
                  pl.BlockSpec((1, tq, 2 * hd), lambda bi, h, i: (bi, i, h)),
                  pl.BlockSpec((1, t, 2 * hd), lambda bi, h, i: (bi, 0, h)),
                  pl.BlockSpec((1, t, vd), lambda bi, h, i: (bi, 0, h))],
        out_specs=pl.BlockSpec((1, tq, vd), lambda bi, h, i: (bi, i, h)),
        out_shape=jax.ShapeDtypeStruct((b, t, n_heads * vd), BF16),
        scratch_shapes=(two((2, tq, tq), F32) + two((2, tq, tq), BF16) + two((2, tq, LANES), F32)
                        + [pltpu.VMEM((2, tq, vd), F32), pltpu.VMEM((2, tq, LANES), F32),
                           pltpu.VMEM((2, tq, LANES), F32)]),
        compiler_params=_params("parallel", "parallel", "arbitrary"),
        name="attn_prompt",
    )(lam_vec, g, q, k, v)


def _attn_sample_kernel(lam_ref, g_ref, q_ref, kn_ref, vn_ref, kp_ref, *rest, n_heads, hd, lam_init):
    vp_refs, o_ref = rest[:-1], rest[-1]
    past = vp_refs[0].shape[1] // n_heads
    vd = LANES * len(vp_refs)
    lam = _lambda_value(lam_ref, lam_init)
    for h in range(n_heads):
        vp = jnp.concatenate([r[0, pl.ds(h, past, stride=n_heads), :].astype(BF16) for r in vp_refs],
                             axis=1)
        vn = vn_ref[0, :, h * vd:(h + 1) * vd]
        outs = []
        for comp in range(2):
            col = (2 * h + comp) * hd
            q = q_ref[0, :, col:col + hd]
            kp = kp_ref[0, pl.ds(2 * h + comp, past, stride=2 * n_heads), :].astype(BF16)
            sp = _dot_nt(q, kp)
            sn = _dot_nt(q, kn_ref[0, :, col:col + hd])
            m = jnp.maximum(jnp.max(sp, axis=1, keepdims=True), jnp.max(sn, axis=1, keepdims=True))
            pp = jnp.exp2(sp - m)
            pn = jnp.exp2(sn - m)
            l = jnp.sum(pp, axis=1, keepdims=True) + jnp.sum(pn, axis=1, keepdims=True)
            outs.append((_dot(pp.astype(BF16), vp) + _dot(pn.astype(BF16), vn)) / l)
        o = outs[0] - lam * outs[1]
        o_ref[0, :, h * vd:(h + 1) * vd] = _sub_norm(o, g_ref[...], lam_init).astype(o_ref.dtype)


def _attn_sample(q, k, v, k_past, v_past, layer, lam_vec, g, lam_init, n_heads, hd):
    b, t, _ = q.shape
    vd = v.shape[2] // n_heads
    kern = functools.partial(_attn_sample_kernel, n_heads=n_heads, hd=hd, lam_init=lam_init)
    tok = lambda w: pl.BlockSpec((1, t, w), lambda bi: (bi, 0, 0))
    return pl.pallas_call(
        kern,
        grid=(b,),
        in_specs=[_resident((4, hd), lambda bi: (0, 0)),
                  _resident((1, vd), lambda bi: (0, 0)),
                  tok(q.shape[2]), tok(k.shape[2]), tok(v.shape[2]),
                  pl.BlockSpec((1,) + k_past.shape[1:], lambda bi: (layer * b + bi, 0, 0))]
                 + [pl.BlockSpec((1, v_past.shape[1], LANES), lambda bi, c=c: (layer * b + bi, 0, c))
                    for c in range(vd // LANES)],
        out_specs=tok(n_heads * vd),
        out_shape=jax.ShapeDtypeStruct((b, t, n_heads * vd), BF16),
        compiler_params=_params("parallel"),
        name="attn_sample",
    )(lam_vec, g, q, k, v, k_past, *([v_past] * (vd // LANES)))


def _recurrent_kernel(xr_ref, xg_ref, cbuf_ref, h0_ref, cw_ref, cb_ref, wrg_ref, bra_ref, brx_ref,
                      lam_ref, y_ref, cnew_ref, hlast_ref, xpad_ref, a_ref, u_ref, hs_ref, hc_ref,
                      *, tt, cw, bw):
    i = pl.program_id(1)
    hdr = 8
    d = xr_ref.shape[2]

    @pl.when(i == 0)
    def _():
        xpad_ref[hdr - (cw - 1):hdr, :] = cbuf_ref[0]
        hc_ref[...] = h0_ref[0]

    xpad_ref[hdr:hdr + tt, :] = xr_ref[0]
    xc = cb_ref[...] + xpad_ref[hdr - (cw - 1):hdr - (cw - 1) + tt, :] * cw_ref[0:1, :]
    for j in range(1, cw):
        lo = hdr - (cw - 1) + j
        xc = xc + xpad_ref[lo:lo + tt, :] * cw_ref[j:j + 1, :]
    tail = xpad_ref[hdr + tt - (cw - 1):hdr + tt, :]
    cnew_ref[0] = tail
    xpad_ref[hdr - (cw - 1):hdr, :] = tail

    for n in range(d // bw):
        lo, hi = n * bw, (n + 1) * bw
        xcn = xc[:, lo:hi]
        gates = _dot(xcn.astype(BF16), wrg_ref[n])
        r = jax.nn.sigmoid(gates[:, :bw] + bra_ref[:, lo:hi])
        gi = jax.nn.sigmoid(gates[:, bw:] + brx_ref[:, lo:hi])
        log_a = -RG_C * r * jax.nn.softplus(-lam_ref[:, lo:hi])
        a = jnp.exp(log_a)
        a_ref[:, lo:hi] = a
        u_ref[:, lo:hi] = jnp.sqrt(-jnp.tanh(log_a) * (a * a + 1.0)) * (gi * xcn)

    def scan_row(t, h):
        h = a_ref[pl.ds(t, 1), :] * h + u_ref[pl.ds(t, 1), :]
        hs_ref[pl.ds(t, 1), :] = h
        return h

    h = lax.fori_loop(0, tt, scan_row, hc_ref[...])
    hc_ref[...] = h
    hlast_ref[0] = h
    y_ref[0] = (hs_ref[...] * jax.nn.gelu(xg_ref[0])).astype(y_ref.dtype)


def _recurrent(zrest, b, t, d, conv_buf, h0, conv_w, conv_b, w_rg, b_rg_a, b_rg_x, rg_lambda):
    cw = conv_w.shape[0]
    bw = w_rg.shape[1]
    tt = min(256, t)
    z3 = zrest.reshape(b, t, zrest.shape[1])
    kern = functools.partial(_recurrent_kernel, tt=tt, cw=cw, bw=bw)
    row = lambda bi, i: (0, 0)
    y, cnew, hlast = pl.pallas_call(
        kern,
        grid=(b, t // tt),
        in_specs=[pl.BlockSpec((1, tt, d), lambda bi, i: (bi, i, 0)),
                  pl.BlockSpec((1, tt, d), lambda bi, i: (bi, i, 1)),
                  pl.BlockSpec((1, cw - 1, d), lambda bi, i: (bi, 0, 0)),
                  pl.BlockSpec((1, 1, d), lambda bi, i: (bi, 0, 0)),
                  _resident((cw, d), row),
                  _resident((1, d), row),
                  _resident(w_rg.shape, lambda bi, i: (0, 0, 0)),
                  _resident((1, d), row),
                  _resident((1, d), row),
                  _resident((1, d), row)],
        out_specs=[pl.BlockSpec((1, tt, d), lambda bi, i: (bi, i, 0)),
                   pl.BlockSpec((1, cw - 1, d), lambda bi, i: (bi, 0, 0)),
                   pl.BlockSpec((1, 1, d), lambda bi, i: (bi, 0, 0))],
        out_shape=[jax.ShapeDtypeStruct((b, t, d), BF16),
                   jax.ShapeDtypeStruct((b, cw - 1, d), F32),
                   jax.ShapeDtypeStruct((b, 1, d), F32)],
        scratch_shapes=[pltpu.VMEM((tt + 8, d), F32), pltpu.VMEM((tt, d), F32),
                        pltpu.VMEM((tt, d), F32), pltpu.VMEM((tt, d), F32), pltpu.VMEM((1, d), F32)],
        compiler_params=_params("parallel", "arbitrary"),
        name="recurrent",
    )(z3, z3, conv_buf, h0.reshape(b, 1, d), conv_w, conv_b, w_rg, b_rg_a, b_rg_x, rg_lambda)
    return y, cnew, hlast.reshape(b, d)


def _layer_norm(x, g, b):
    mu = jnp.mean(x, axis=1, keepdims=True)
    xc = x - mu
    var = jnp.mean(xc * xc, axis=1, keepdims=True)
    return xc * lax.rsqrt(var + LN_EPS) * g + b


def _merge_kernel(oa_ref, yb_ref, ga_ref, gr_ref, x_ref, wa_ref, wb_ref, wo_ref, g_ref, b_ref,
                  x1_ref, x1t_ref, *, alpha):
    br_a = _dot(oa_ref[...], wa_ref[...])
    br_b = _dot(yb_ref[...], wb_ref[...])
    m = jax.nn.sigmoid(ga_ref[...]) * br_a + jax.nn.sigmoid(gr_ref[...]) * br_b
    r = _dot(m.astype(BF16), wo_ref[...])
    x1 = _layer_norm(alpha * x_ref[...] + r, g_ref[...], b_ref[...])
    x1_ref[...] = x1
    x1t_ref[...] = x1.T.astype(BF16)


def _merge(oa, yb, zrest, x2d, w_attn, w_rec, w_out, ln_g, ln_b, alpha):
    n, d = x2d.shape
    tm = min(256, n)
    tok = lambda c: pl.BlockSpec((tm, d), lambda i: (i, c))
    const = lambda i: (0, 0)
    return pl.pallas_call(
        functools.partial(_merge_kernel, alpha=alpha),
        grid=(n // tm,),
        in_specs=[pl.BlockSpec((tm, oa.shape[1]), lambda i: (i, 0)), tok(0), tok(2), tok(3), tok(0),
                  _resident(w_attn.shape, const), _resident(w_rec.shape, const),
                  _resident(w_out.shape, const), _resident((1, d), const), _resident((1, d), const)],
        out_specs=[tok(0), pl.BlockSpec((d, tm), lambda i: (0, i))],
        out_shape=[jax.ShapeDtypeStruct((n, d), F32), jax.ShapeDtypeStruct((d, n), BF16)],
        compiler_params=_params("parallel"),
        name="merge",
    )(oa, yb, zrest, zrest, x2d, w_attn, w_rec, w_out, ln_g, ln_b)


def _top_rows(x, k, fill):
    rows = []
    for it in range(k):
        m = jnp.max(x, axis=0, keepdims=True)
        rows.append(m)
        if it + 1 < k:
            x = jnp.where(x == m, fill, x)
    return rows


def _candidate_pairs(k):
    return [(a, b) for a in range(k) for b in range(k) if (a + 1) * (b + 1) <= k]


def _peer_route_kernel(xt_ref, wq_ref, sk1_ref, sk2_ref, e1_ref, e2_ref, pt_ref, *, heads, half):
    qt = _dot(wq_ref[...], xt_ref[...]).astype(BF16)
    tm = qt.shape[1]
    pairs = _candidate_pairs(TOPK)
    pad_rows = (-len(pairs)) % 8
    pad = [jnp.full((pad_rows, tm), -1.0, F32)] if pad_rows else []

    def pair_products(ea, eb):
        return jnp.concatenate([ea[a] * eb[b] for a, b in pairs] + pad, axis=0)

    for h in range(heads):
        base = h * 2 * half
        s1 = _dot(sk1_ref[...], qt[base:base + half])
        s2 = _dot(sk2_ref[...], qt[base + half:base + 2 * half])
        v1 = _top_rows(s1, TOPK, -jnp.inf)
        v2 = _top_rows(s2, TOPK, -jnp.inf)
        e1 = jnp.exp(s1 - v1[0])
        e2 = jnp.exp(s2 - v2[0])
        e1_top = [jnp.exp(v - v1[0]) for v in v1]
        e2_top = [jnp.exp(v - v2[0]) for v in v2]
        tops = _top_rows(pair_products(e1_top, e2_top), TOPK, -1.0)
        z = functools.reduce(lambda acc, r: acc + jnp.maximum(r, 0.0), tops[1:], tops[0])
        rz = 1.0 / z
        e1n = e1 * rz
        e1n_top = [e * rz for e in e1_top]
        thr = _top_rows(pair_products(e1n_top, e2_top), TOPK, -1.0)[-1]
        e1_ref[h] = e1n
        e2_ref[h] = e2
        pt_ref[h:h + 1, :] = jnp.maximum(thr, 0.0)


def _peer_route(x1t, wq_t, sk1, sk2, heads):
    d, n = x1t.shape
    nk, half = sk1.shape
    tm = min(512, n)
    const = lambda i: (0, 0)
    return pl.pallas_call(
        functools.partial(_peer_route_kernel, heads=heads, half=half),
        grid=(n // tm,),
        in_specs=[pl.BlockSpec((d, tm), lambda i: (0, i)),
                  _resident(wq_t.shape, const), _resident(sk1.shape, const), _resident(sk2.shape, const)],
        out_specs=[pl.BlockSpec((heads, nk, tm), lambda i: (0, 0, i)),
                   pl.BlockSpec((heads, nk, tm), lambda i: (0, 0, i)),
                   pl.BlockSpec((heads, tm), lambda i: (0, i))],
        out_shape=[jax.ShapeDtypeStruct((heads, nk, n), F32),
                   jax.ShapeDtypeStruct((heads, nk, n), F32),
                   jax.ShapeDtypeStruct((heads, n), F32)],
        compiler_params=_params("parallel"),
        name="peer_route",
    )(x1t, wq_t, sk1, sk2)


def _peer_dense_kernel(xt_ref, u_ref, vt_ref, e1_ref, e2_ref, pt_ref, x1_ref, g_ref, b_ref, y_ref, acc_ref,
                       *, heads, nk, rows_per_step, alpha):
    j = pl.program_id(1)

    @pl.when(j == 0)
    def _():
        acc_ref[...] = jnp.zeros_like(acc_ref)

    act = jax.nn.gelu(_dot(u_ref[...], xt_ref[...]))
    parts = []
    for a in range(rows_per_step):
        w = None
        for h in range(heads):
            p = e1_ref[h, a:a + 1, :] * e2_ref[h]
            sel = jnp.where(p >= pt_ref[h:h + 1, :], p, 0.0)
            w = sel if w is None else w + sel
        parts.append((w * act[a * nk:(a + 1) * nk]).astype(BF16))
    wa = jnp.concatenate(parts, axis=0) if len(parts) > 1 else parts[0]
    acc_ref[...] += _dot(vt_ref[...], wa)

    @pl.when(j == pl.num_programs(1) - 1)
    def _():
        y_ref[...] = _layer_norm(alpha * x1_ref[...] + acc_ref[...].T, g_ref[...], b_ref[...])


def _peer_dense(x1, x1t, u, vt, e1, e2, pt, ln_g, ln_b, alpha):
    d, n = x1t.shape
    heads, nk, _ = e1.shape
    ne = u.shape[0]
    tm = min(512, n)
    rows_per_step = 8
    ec = rows_per_step * nk
    kern = functools.partial(_peer_dense_kernel, heads=heads, nk=nk, rows_per_step=rows_per_step, alpha=alpha)
    return pl.pallas_call(
        kern,
        grid=(n // tm, ne // ec),
        in_specs=[_resident((d, tm), lambda i, j: (0, i)),
                  pl.BlockSpec((ec, d), lambda i, j: (j, 0)),
                  pl.BlockSpec((d, ec), lambda i, j: (0, j)),
                  pl.BlockSpec((heads, rows_per_step, tm), lambda i, j: (0, j, i)),
                  _resident((heads, nk, tm), lambda i, j: (0, 0, i)),
                  _resident((heads, tm), lambda i, j: (0, i)),
                  _resident((tm, d), lambda i, j: (i, 0)),
                  _resident((1, d), lambda i, j: (0, 0)),
                  _resident((1, d), lambda i, j: (0, 0))],
        out_specs=pl.BlockSpec((tm, d), lambda i, j: (i, 0)),
        out_shape=jax.ShapeDtypeStruct((n, d), F32),
        scratch_shapes=[pltpu.VMEM((d, tm), F32)],
        compiler_params=_params("parallel", "arbitrary"),
        name="peer_dense",
    )(x1t, u, vt, e1, e2, pt, x1, ln_g, ln_b)


def _cast_kernel(w_ref, o_ref, *, transpose):
    w = w_ref[...]
    o_ref[...] = (w.T if transpose else w).astype(o_ref.dtype)


def _cast_table(w3d, layer, *, transpose):
    _, rows, cols = w3d.shape
    tr, tc = min(512, rows), min(2048, cols)
    out_shape = (cols, rows) if transpose else (rows, cols)
    out_block = (tc, tr) if transpose else (tr, tc)
    out_map = (lambda i, j: (j, i)) if transpose else (lambda i, j: (i, j))
    return pl.pallas_call(
        functools.partial(_cast_kernel, transpose=transpose),
        grid=(rows // tr, cols // tc),
        in_specs=[pl.BlockSpec((None, tr, tc), lambda i, j: (layer, i, j))],
        out_specs=pl.BlockSpec(out_block, out_map),
        out_shape=jax.ShapeDtypeStruct(out_shape, BF16),
        compiler_params=_params("parallel", "parallel"),
        name="cast_table",
    )(w3d)


def _prepare_weights(w):
    (w_in, b_in, lambda_q1, lambda_k1, lambda_q2, lambda_k2, subln_g, w_attn_branch,
     conv_w, conv_b, w_rg_a, b_rg_a, w_rg_x, b_rg_x, rg_lambda, w_rec_branch, w_out,
     ln1_g, ln1_b, w_query, sub_keys1, sub_keys2, peer_u, peer_v, ln2_g, ln2_b) = w
    d = w_in.shape[0]
    hd = lambda_q1.shape[0]
    vd = subln_g.shape[0]
    v_w = w_attn_branch.shape[0]
    n_heads = v_w // vd
    qk_w = n_heads * 2 * hd
    row = lambda a: a.reshape(1, -1).astype(F32)
    cuts = (0, qk_w, 2 * qk_w, 2 * qk_w + v_w, w_in.shape[1])
    return dict(
        d=d, hd=hd, vd=vd, n_heads=n_heads, cuts=cuts,
        w_in=w_in.astype(BF16), b_in=row(b_in),
        lam_vec=jnp.stack([lambda_q1, lambda_k1, lambda_q2, lambda_k2]).astype(F32),
        subln_g=row(subln_g),
        w_attn=w_attn_branch.astype(BF16), w_rec=w_rec_branch.astype(BF16), w_out=w_out.astype(BF16),
        conv_w=conv_w.astype(F32), conv_b=row(conv_b),
        w_rg=jnp.concatenate([w_rg_a, w_rg_x], axis=-1).astype(BF16),
        b_rg_a=row(b_rg_a), b_rg_x=row(b_rg_x), rg_lambda=row(rg_lambda),
        ln1_g=row(ln1_g), ln1_b=row(ln1_b), ln2_g=row(ln2_g), ln2_b=row(ln2_b),
        wq_t=w_query.T.astype(BF16), sk1=sub_keys1.astype(BF16), sk2=sub_keys2.astype(BF16),
        peer_heads=w_query.shape[1] // (2 * sub_keys1.shape[1]),
    )


def _layer(x, pos, caches, layer, conv_buf, h0, lam_init, alpha, p):
    b, t, d = x.shape
    n = b * t
    hd, vd, n_heads = p["hd"], p["vd"], p["n_heads"]
    x2d = x.reshape(n, d)
    assert hd == LANES, "the rotary epilogue maps one q/k half onto one 128-lane tile"

    rope = _rope_tables(pos, b, hd)
    cuts, w_in, b_in = p["cuts"], p["w_in"], p["b_in"]
    width = lambda s: cuts[s + 1] - cuts[s]
    q_b, x_b = _project(x2d, w_in, b_in, cuts[0], width(0), rope=rope, out_scale=hd ** -0.5 * LOG2E,
                        emit_f32=False, emit_bf16=True, emit_xb=True, tm=512, tn=width(0))
    k_f, k_b = _project(x_b, w_in, b_in, cuts[1], width(1), rope=rope, emit_bf16=True, split_rows=True,
                        tm=512, tn=width(1))
    v_f, v_b = _project(x_b, w_in, b_in, cuts[2], width(2), emit_bf16=True, tm=512, tn=width(2))
    (zrest,) = _project(x_b, w_in, b_in, cuts[3], width(3))

    q3, k3, v3 = (a.reshape(b, t, -1) for a in (q_b, k_b, v_b))
    if caches is None:
        oa = _attn_prompt(q3, k3, v3, p["lam_vec"], p["subln_g"], lam_init, n_heads, hd)
    else:
        oa = _attn_sample(q3, k3, v3, caches[0], caches[1], layer, p["lam_vec"], p["subln_g"], lam_init,
                          n_heads, hd)

    yb, conv_new, h_last = _recurrent(zrest, b, t, d, conv_buf, h0, p["conv_w"], p["conv_b"], p["w_rg"],
                                      p["b_rg_a"], p["b_rg_x"], p["rg_lambda"])

    x1, x1t = _merge(oa.reshape(n, -1), yb.reshape(n, d), zrest, x2d, p["w_attn"], p["w_rec"], p["w_out"],
                     p["ln1_g"], p["ln1_b"], alpha)

    e1, e2, pt = _peer_route(x1t, p["wq_t"], p["sk1"], p["sk2"], p["peer_heads"])
    y = _peer_dense(x1, x1t, p["peer_u"], p["peer_vt"], e1, e2, pt, p["ln2_g"], p["ln2_b"], alpha)

    return (y.reshape(b, t, d), k_f.reshape(b, t, n_heads, 2, hd), v_f.reshape(b, t, n_heads, vd),
            conv_new, h_last)


def kernel(x_prompt, x_sample, cache_k, cache_v, state_conv, state_h, w_in, b_in, lambda_q1, lambda_k1, lambda_q2, lambda_k2, subln_g, w_attn_branch, conv_w, conv_b, w_rg_a, b_rg_a, w_rg_x, b_rg_x, rg_lambda, w_rec_branch, w_out, ln1_g, ln1_b, w_query, sub_keys1, sub_keys2, peer_u, peer_v, ln2_g, ln2_b):
    layer_weights = (w_in, b_in, lambda_q1, lambda_k1, lambda_q2, lambda_k2, subln_g,
                     w_attn_branch, conv_w, conv_b, w_rg_a, b_rg_a, w_rg_x, b_rg_x, rg_lambda,
                     w_rec_branch, w_out, ln1_g, ln1_b, w_query, sub_keys1, sub_keys2,
                     peer_u, peer_v, ln2_g, ln2_b)
    depth = w_in.shape[0]
    alpha = (2.0 * depth) ** 0.25
    past_len = cache_k.shape[2]
    d_rnn = conv_w.shape[2]
    pos_p = jnp.arange(x_prompt.shape[1])
    pos_s = past_len + jnp.arange(x_sample.shape[1])
    conv0 = jnp.zeros((x_prompt.shape[0], conv_w.shape[1] - 1, d_rnn), x_prompt.dtype)
    h_zero = jnp.zeros((x_prompt.shape[0], d_rnn), x_prompt.dtype)
    caches = (cache_k.reshape(-1, math.prod(cache_k.shape[2:5]), cache_k.shape[5]),
              cache_v.reshape(-1, math.prod(cache_v.shape[2:4]), cache_v.shape[4]))
    yp, ys = x_prompt, x_sample
    outs_p, outs_s = [], []
    for l in range(depth):
        lam_init = 0.8 - 0.6 * math.exp(-0.3 * l)
        p = _prepare_weights(tuple(wt[l] for wt in layer_weights))
        p["peer_u"] = _cast_table(peer_u, l, transpose=False)
        p["peer_vt"] = _cast_table(peer_v, l, transpose=True)
        yp, *rest_p = _layer(yp, pos_p, None, l, conv0, h_zero, lam_init, alpha, p)
        ys, *rest_s = _layer(ys, pos_s, caches, l, state_conv[l], state_h[l], lam_init, alpha, p)
        outs_p.append(rest_p)
        outs_s.append(rest_s)
    stack = lambda outs, idx: jnp.stack([o[idx] for o in outs])
    return (yp, ys, stack(outs_p, 0), stack(outs_p, 1), stack(outs_p, 2), stack(outs_p, 3),
            stack(outs_s, 0), stack(outs_s, 1), stack(outs_s, 2), stack(outs_s, 3))
```

```python
import functools
import math

import jax
import jax.numpy as jnp
from jax import lax
from jax.experimental import pallas as pl
from jax.experimental.pallas import tpu as pltpu

F32 = jnp.float32
BF16 = jnp.bfloat16

CHUNK = 64
ROPE_THETA = 500000.0
RG_C = 8.0
TOPK = 16
LN_EPS = 1e-5
NEG_INF = -1e30
LOG2E = 1.4426950408889634

LANES = 128
VMEM_LIMIT_BYTES = 56 * 1024 * 1024


def _params(*sem):
    return pltpu.CompilerParams(dimension_semantics=sem, vmem_limit_bytes=VMEM_LIMIT_BYTES)


def _resident(shape, index_map):
    return pl.BlockSpec(shape, index_map, pipeline_mode=pl.Buffered(1))


def _dot(a, b):
    return jnp.dot(a, b, preferred_element_type=F32)


def _dot_nt(a, b):
    return lax.dot_general(a, b, (((1,), (1,)), ((), ())), preferred_element_type=F32)


def _proj_kernel(*refs, rope_shift, out_scale, emit_f32, emit_bf16, emit_xb, split_rows):
    it = iter(refs)
    x_ref, w_ref, b_ref = next(it), next(it), next(it)
    if rope_shift:
        c_ref, sa_ref, sb_ref = next(it), next(it), next(it)
    of_ref = next(it) if emit_f32 else None
    ob_ref = next(it) if emit_bf16 else None
    xb_ref = next(it) if emit_xb else None

    x = x_ref[...].astype(BF16)
    if emit_xb:
        xb_ref[...] = x
    z = _dot(x, w_ref[...]) + b_ref[...]
    n_tiles = z.shape[1] // LANES
    if rope_shift:
        c, sa, sb = c_ref[...], sa_ref[...], sb_ref[...]
        parts = []
        for s in range(n_tiles):
            t = z[:, s * LANES:(s + 1) * LANES]
            parts.append(t * c + pltpu.roll(t, rope_shift, 1) * sa
                         + pltpu.roll(t, LANES - rope_shift, 1) * sb)
        z = jnp.concatenate(parts, axis=1) if len(parts) > 1 else parts[0]
    if out_scale != 1.0:
        z = z * out_scale
    if emit_f32 and split_rows:
        for s in range(n_tiles):
            of_ref[pl.ds(s, z.shape[0], stride=n_tiles), :] = z[:, s * LANES:(s + 1) * LANES]
    elif emit_f32:
        of_ref[...] = z
    if emit_bf16:
        ob_ref[...] = z.astype(BF16)


def _project(x2d, w, b, col0, nout, *, rope=None, out_scale=1.0, emit_f32=True, emit_bf16=False,
             emit_xb=False, split_rows=False, tm=1024, tn=512):
    n, kdim = x2d.shape
    tm, tn = min(tm, n), math.gcd(min(tn, nout), col0)
    assert not (split_rows and tn != nout)
    jb = col0 // tn
    w_spec = (_resident((kdim, tn), lambda i, j: (0, jb + j)) if tn == nout
              else pl.BlockSpec((kdim, tn), lambda i, j: (0, jb + j)))
    in_specs = [pl.BlockSpec((tm, kdim), lambda i, j: (i, 0)), w_spec,
                pl.BlockSpec((1, tn), lambda i, j: (0, jb + j))]
    args = [x2d, w, b]
    rope_shift = 0
    if rope is not None:
        rope_shift, tabs = rope
        in_specs += [pl.BlockSpec((tm, LANES), lambda i, j: (i, 0))] * 3
        args += list(tabs)
    out_shape, out_specs = [], []
    if emit_f32 and split_rows:
        tiles = nout // LANES
        out_shape.append(jax.ShapeDtypeStruct((n * tiles, LANES), F32))
        out_specs.append(pl.BlockSpec((tm * tiles, LANES), lambda i, j: (i, 0)))
    elif emit_f32:
        out_shape.append(jax.ShapeDtypeStruct((n, nout), F32))
        out_specs.append(pl.BlockSpec((tm, tn), lambda i, j: (i, j)))
    if emit_bf16:
        out_shape.append(jax.ShapeDtypeStruct((n, nout), BF16))
        out_specs.append(pl.BlockSpec((tm, tn), lambda i, j: (i, j)))
    if emit_xb:
        out_shape.append(jax.ShapeDtypeStruct((n, kdim), BF16))
        out_specs.append(pl.BlockSpec((tm, kdim), lambda i, j: (i, 0)))
    return pl.pallas_call(
        functools.partial(_proj_kernel, rope_shift=rope_shift, out_scale=out_scale, emit_f32=emit_f32,
                          emit_bf16=emit_bf16, emit_xb=emit_xb, split_rows=split_rows),
        grid=(n // tm, nout // tn),
        in_specs=in_specs, out_specs=out_specs, out_shape=out_shape,
        compiler_params=_params("parallel", "arbitrary"),
        name="in_proj",
    )(*args)


def _rope_tables(pos, batch, head_dim):
    rope_dim = head_dim // 4
    half = rope_dim // 2
    inv = ROPE_THETA ** (-jnp.arange(half, dtype=F32) / half)
    ang = pos.astype(F32)[:, None] * inv[None, :]
    cos, sin = jnp.cos(ang), jnp.sin(ang)
    t = pos.shape[0]
    ones = jnp.ones((t, head_dim - rope_dim), F32)
    zeros_h = jnp.zeros((t, half), F32)
    zeros_r = jnp.zeros((t, head_dim - rope_dim), F32)
    c = jnp.concatenate([cos, cos, ones], axis=1)
    sa = jnp.concatenate([zeros_h, sin, zeros_r], axis=1)
    sb = jnp.concatenate([-sin, zeros_h, zeros_r], axis=1)
    return half, tuple(jnp.tile(a, (batch, 1)) for a in (c, sa, sb))


def _lambda_value(lam_ref, lam_init):
    lv = lam_ref[...]
    s1 = jnp.sum(lv[0:1] * lv[1:2], axis=1, keepdims=True)
    s2 = jnp.sum(lv[2:3] * lv[3:4], axis=1, keepdims=True)
    return jnp.exp(s1) - jnp.exp(s2) + lam_init


def _sub_norm(o, g, lam_init):
    ms = jnp.mean(o * o, axis=1, keepdims=True)
    return o * lax.rsqrt(ms + LN_EPS) * g * (1.0 - lam_init)


def _attn_prompt_kernel(lam_ref, g_ref, q_ref, k_ref, v_ref, o_ref, s0_ref, s1_ref, p0_ref, p1_ref,
                        a0_ref, a1_ref, acc_ref, m_ref, l_ref, *, tq, hd, lam_init, sub):
    i = pl.program_id(2)
    vd = v_ref.shape[2]
    n_sub = tq // sub
    lane_tiles = tq // LANES
    s_refs, p_refs, a_refs = (s0_ref, s1_ref), (p0_ref, p1_ref), (a0_ref, a1_ref)

    def scores(j, par):
        off = pl.multiple_of(j * tq, tq)
        for comp in range(2):
            s_refs[par][comp] = _dot_nt(q_ref[0, :, comp * hd:(comp + 1) * hd],
                                        k_ref[0, pl.ds(off, tq), comp * hd:(comp + 1) * hd])

    def softmax(par, masked):
        for comp in range(2):
            for r in range(n_sub):
                rows = slice(r * sub, (r + 1) * sub)
                s = s_refs[par][comp, rows, :]
                if masked:
                    q_chunk = (lax.broadcasted_iota(jnp.int32, (sub, tq), 0) + r * sub) // CHUNK
                    k_chunk = lax.broadcasted_iota(jnp.int32, (sub, tq), 1) // CHUNK
                    s = jnp.where(k_chunk <= q_chunk, s, NEG_INF)
                m_prev = m_ref[comp, rows, :]
                m_next = jnp.maximum(m_prev, jnp.max(s, axis=1, keepdims=True))
                alpha = jnp.exp2(m_prev - m_next)
                p = jnp.exp2(s - jnp.tile(m_next, (1, lane_tiles)))
                psum = p[:, 0:LANES]
                for t in range(1, lane_tiles):
                    psum = psum + p[:, t * LANES:(t + 1) * LANES]
                l_ref[comp, rows, :] = alpha * l_ref[comp, rows, :] + psum
                m_ref[comp, rows, :] = m_next
                a_refs[par][comp, rows, :] = alpha
                p_refs[par][comp, rows, :] = p.astype(BF16)

    def accumulate(j, par):
        off = pl.multiple_of(j * tq, tq)
        v = v_ref[0, pl.ds(off, tq), :]
        for comp in range(2):
            acc_ref[comp] = (acc_ref[comp] * jnp.tile(a_refs[par][comp], (1, vd // LANES))
                             + _dot(p_refs[par][comp], v))

    def pipelined_step(j, par):
        accumulate(j - 1, 1 - par)
        softmax(par, False)
        scores(j + 1, 1 - par)

    acc_ref[...] = jnp.zeros_like(acc_ref)
    m_ref[...] = jnp.full_like(m_ref, NEG_INF)
    l_ref[...] = jnp.zeros_like(l_ref)

    scores(0, 0)

    @pl.when(i > 0)
    def _():
        softmax(0, False)
        scores(1, 1)

    def pair(t, carry):
        pipelined_step(2 * t + 1, 1)
        pipelined_step(2 * t + 2, 0)
        return carry

    lax.fori_loop(0, (i - 1) // 2, pair, 0)

    @pl.when(jnp.logical_and(i > 1, i % 2 == 0))
    def _():
        pipelined_step(i - 1, 1)

    @pl.when(i % 2 == 0)
    def _():
        @pl.when(i > 0)
        def _():
            accumulate(i - 1, 1)
        softmax(0, True)
        accumulate(i, 0)

    @pl.when(i % 2 == 1)
    def _():
        accumulate(i - 1, 0)
        softmax(1, True)
        accumulate(i, 1)

    lam = _lambda_value(lam_ref, lam_init)
    l1 = jnp.sum(l_ref[0], axis=1, keepdims=True)
    l2 = jnp.sum(l_ref[1], axis=1, keepdims=True)
    o = acc_ref[0] / l1 - lam * (acc_ref[1] / l2)
    o_ref[0] = _sub_norm(o, g_ref[...], lam_init).astype(o_ref.dtype)


def _attn_prompt(q, k, v, lam_vec, g, lam_init, n_heads, hd):
    b, t, _ = q.shape
    vd = v.shape[2] // n_heads
    tq = min(512, t)
    sub = min(64, tq)
    kern = functools.partial(_attn_prompt_kernel, tq=tq, hd=hd, lam_init=lam_init, sub=sub)
    two = lambda shape, dt: [pltpu.VMEM(shape, dt), pltpu.VMEM(shape, dt)]
    return pl.pallas_call(
        kern,
        grid=(b, n_heads, t // tq),
        in_specs=[_resident((4, hd), lambda bi, h, i: (0, 0)),
                  _resident((1, vd), lambda bi, h, i: (0, 0)),
                  pl.BlockSpec((1, tq, 2 * hd), lambda bi, h, i: (bi, i, h)),
                  pl.BlockSpec((1, t, 2 * hd), lambda bi, h, i: (bi, 0, h)),
                  pl.BlockSpec((1, t, vd), lambda bi, h, i: (bi, 0, h))],
        out_specs=pl.BlockSpec((1, tq, vd), lambda bi, h, i: (bi, i, h)),
        out_shape=jax.ShapeDtypeStruct((b, t, n_heads * vd), BF16),
        scratch_shapes=(two((2, tq, tq), F32) + two((2, tq, tq), BF16) + two((2, tq, LANES), F32)
                        + [pltpu.VMEM((2, tq, vd), F32), pltpu.VMEM((2, tq, LANES), F32),
                           pltpu.VMEM((2, tq, LANES), F32)]),
        compiler_params=_params("parallel", "parallel", "arbitrary"),
        name="attn_prompt",
    )(lam_vec, g, q, k, v)


def _attn_sample_kernel(lam_ref, g_ref, q_ref, kn_ref, vn_ref, kp_ref, *rest, n_heads, hd, lam_init):
    vp_refs, o_ref = rest[:-1], rest[-1]
    past = vp_refs[0].shape[1] // n_heads
    vd = LANES * len(vp_refs)
    lam = _lambda_value(lam_ref, lam_init)
    for h in range(n_heads):
        vp = jnp.concatenate([r[0, pl.ds(h, past, stride=n_heads), :].astype(BF16) for r in vp_refs],
                             axis=1)
        vn = vn_ref[0, :, h * vd:(h + 1) * vd]
        outs = []
        for comp in range(2):
            col = (2 * h + comp) * hd
            q = q_ref[0, :, col:col + hd]
            kp = kp_ref[0, pl.ds(2 * h + comp, past, stride=2 * n_heads), :].astype(BF16)
            sp = _dot_nt(q, kp)
            sn = _dot_nt(q, kn_ref[0, :, col:col + hd])
            m = jnp.maximum(jnp.max(sp, axis=1, keepdims=True), jnp.max(sn, axis=1, keepdims=True))
            pp = jnp.exp2(sp - m)
            pn = jnp.exp2(sn - m)
            l = jnp.sum(pp, axis=1, keepdims=True) + jnp.sum(pn, axis=1, keepdims=True)
            outs.append((_dot(pp.astype(BF16), vp) + _dot(pn.astype(BF16), vn)) / l)
        o = outs[0] - lam * outs[1]
        o_ref[0, :, h * vd:(h + 1) * vd] = _sub_norm(o, g_ref[...], lam_init).astype(o_ref.dtype)


def _attn_sample(q, k, v, k_past, v_past, layer, lam_vec, g, lam_init, n_heads, hd):
    b, t, _ = q.shape
    vd = v.shape[2] // n_heads
    kern = functools.partial(_attn_sample_kernel, n_heads=n_heads, hd=hd, lam_init=lam_init)
    tok = lambda w: pl.BlockSpec((1, t, w), lambda bi: (bi, 0, 0))
    return pl.pallas_call(
        kern,
        grid=(b,),
        in_specs=[_resident((4, hd), lambda bi: (0, 0)),
                  _resident((1, vd), lambda bi: (0, 0)),
                  tok(q.shape[2]), tok(k.shape[2]), tok(v.shape[2]),
                  pl.BlockSpec((1,) + k_past.shape[1:], lambda bi: (layer * b + bi, 0, 0))]
                 + [pl.BlockSpec((1, v_past.shape[1], LANES), lambda bi, c=c: (layer * b + bi, 0, c))
                    for c in range(vd // LANES)],
        out_specs=tok(n_heads * vd),
        out_shape=jax.ShapeDtypeStruct((b, t, n_heads * vd), BF16),
        compiler_params=_params("parallel"),
        name="attn_sample",
    )(lam_vec, g, q, k, v, k_past, *([v_past] * (vd // LANES)))


def _recurrent_kernel(xr_ref, xg_ref, cbuf_ref, h0_ref, cw_ref, cb_ref, wrg_ref, bra_ref, brx_ref,
                      lam_ref, y_ref, cnew_ref, hlast_ref, xpad_ref, a_ref, u_ref, hs_ref, hc_ref,
                      *, tt, cw, bw):
    i = pl.program_id(1)
    hdr = 8
    d = xr_ref.shape[2]

    @pl.when(i == 0)
    def _():
        xpad_ref[hdr - (cw - 1):hdr, :] = cbuf_ref[0]
        hc_ref[...] = h0_ref[0]

    xpad_ref[hdr:hdr + tt, :] = xr_ref[0]
    xc = cb_ref[...] + xpad_ref[hdr - (cw - 1):hdr - (cw - 1) + tt, :] * cw_ref[0:1, :]
    for j in range(1, cw):
        lo = hdr - (cw - 1) + j
        xc = xc + xpad_ref[lo:lo + tt, :] * cw_ref[j:j + 1, :]
    tail = xpad_ref[hdr + tt - (cw - 1):hdr + tt, :]
    cnew_ref[0] = tail
    xpad_ref[hdr - (cw - 1):hdr, :] = tail

    for n in range(d // bw):
        lo, hi = n * bw, (n + 1) * bw
        xcn = xc[:, lo:hi]
        gates = _dot(xcn.astype(BF16), wrg_ref[n])
        r = jax.nn.sigmoid(gates[:, :bw] + bra_ref[:, lo:hi])
        gi = jax.nn.sigmoid(gates[:, bw:] + brx_ref[:, lo:hi])
        log_a = -RG_C * r * jax.nn.softplus(-lam_ref[:, lo:hi])
        a = jnp.exp(log_a)
        a_ref[:, lo:hi] = a
        u_ref[:, lo:hi] = jnp.sqrt(-jnp.tanh(log_a) * (a * a + 1.0)) * (gi * xcn)

    def scan_row(t, h):
        h = a_ref[pl.ds(t, 1), :] * h + u_ref[pl.ds(t, 1), :]
        hs_ref[pl.ds(t, 1), :] = h
        return h

    h = lax.fori_loop(0, tt, scan_row, hc_ref[...])
    hc_ref[...] = h
    hlast_ref[0] = h
    y_ref[0] = (hs_ref[...] * jax.nn.gelu(xg_ref[0])).astype(y_ref.dtype)


def _recurrent(zrest, b, t, d, conv_buf, h0, conv_w, conv_b, w_rg, b_rg_a, b_rg_x, rg_lambda):
    cw = conv_w.shape[0]
    bw = w_rg.shape[1]
    tt = min(256, t)
    z3 = zrest.reshape(b, t, zrest.shape[1])
    kern = functools.partial(_recurrent_kernel, tt=tt, cw=cw, bw=bw)
    row = lambda bi, i: (0, 0)
    y, cnew, hlast = pl.pallas_call(
        kern,
        grid=(b, t // tt),
        in_specs=[pl.BlockSpec((1, tt, d), lambda bi, i: (bi, i, 0)),
                  pl.BlockSpec((1, tt, d), lambda bi, i: (bi, i, 1)),
                  pl.BlockSpec((1, cw - 1, d), lambda bi, i: (bi, 0, 0)),
                  pl.BlockSpec((1, 1, d), lambda bi, i: (bi, 0, 0)),
                  _resident((cw, d), row),
                  _resident((1, d), row),
                  _resident(w_rg.shape, lambda bi, i: (0, 0, 0)),
                  _resident((1, d), row),
                  _resident((1, d), row),
                  _resident((1, d), row)],
        out_specs=[pl.BlockSpec((1, tt, d), lambda bi, i: (bi, i, 0)),
                   pl.BlockSpec((1, cw - 1, d), lambda bi, i: (bi, 0, 0)),
                   pl.BlockSpec((1, 1, d), lambda bi, i: (bi, 0, 0))],
        out_shape=[jax.ShapeDtypeStruct((b, t, d), BF16),
                   jax.ShapeDtypeStruct((b, cw - 1, d), F32),
                   jax.ShapeDtypeStruct((b, 1, d), F32)],
        scratch_shapes=[pltpu.VMEM((tt + 8, d), F32), pltpu.VMEM((tt, d), F32),
                        pltpu.VMEM((tt, d), F32), pltpu.VMEM((tt, d), F32), pltpu.VMEM((1, d), F32)],
        compiler_params=_params("parallel", "arbitrary"),
        name="recurrent",
    )(z3, z3, conv_buf, h0.reshape(b, 1, d), conv_w, conv_b, w_rg, b_rg_a, b_rg_x, rg_lambda)
    return y, cnew, hlast.reshape(b, d)


def _layer_norm(x, g, b):
    mu = jnp.mean(x, axis=1, keepdims=True)
    xc = x - mu
    var = jnp.mean(xc * xc, axis=1, keepdims=True)
    return xc * lax.rsqrt(var + LN_EPS) * g + b


def _merge_kernel(oa_ref, yb_ref, ga_ref, gr_ref, x_ref, wa_ref, wb_ref, wo_ref, g_ref, b_ref,
                  x1_ref, x1t_ref, *, alpha):
    br_a = _dot(oa_ref[...], wa_ref[...])
    br_b = _dot(yb_ref[...], wb_ref[...])
    m = jax.nn.sigmoid(ga_ref[...]) * br_a + jax.nn.sigmoid(gr_ref[...]) * br_b
    r = _dot(m.astype(BF16), wo_ref[...])
    x1 = _layer_norm(alpha * x_ref[...] + r, g_ref[...], b_ref[...])
    x1_ref[...] = x1
    x1t_ref[...] = x1.T.astype(BF16)


def _merge(oa, yb, zrest, x2d, w_attn, w_rec, w_out, ln_g, ln_b, alpha):
    n, d = x2d.shape
    tm = min(256, n)
    tok = lambda c: pl.BlockSpec((tm, d), lambda i: (i, c))
    const = lambda i: (0, 0)
    return pl.pallas_call(
        functools.partial(_merge_kernel, alpha=alpha),
        grid=(n // tm,),
        in_specs=[pl.BlockSpec((tm, oa.shape[1]), lambda i: (i, 0)), tok(0), tok(2), tok(3), tok(0),
                  _resident(w_attn.shape, const), _resident(w_rec.shape, const),
                  _resident(w_out.shape, const), _resident((1, d), const), _resident((1, d), const)],
        out_specs=[tok(0), pl.BlockSpec((d, tm), lambda i: (0, i))],
        out_shape=[jax.ShapeDtypeStruct((n, d), F32), jax.ShapeDtypeStruct((d, n), BF16)],
        compiler_params=_params("parallel"),
        name="merge",
    )(oa, yb, zrest, zrest, x2d, w_attn, w_rec, w_out, ln_g, ln_b)


def _top_rows(x, k, fill):
    rows = []
    for it in range(k):
        m = jnp.max(x, axis=0, keepdims=True)
        rows.append(m)
        if it + 1 < k:
            x = jnp.where(x == m, fill, x)
    return rows


def _sort_network(n):
    def merge(lo, hi, r):
        step = r * 2
        if step < hi - lo:
            yield from merge(lo, hi, step)
            yield from merge(lo + r, hi, step)
            yield from ((i, i + r) for i in range(lo + r, hi - r, step))
        else:
            yield (lo, lo + r)

    def sort(lo, hi):
        if hi > lo:
            mid = lo + (hi - lo) // 2
            yield from sort(lo, mid)
            yield from sort(mid + 1, hi)
            yield from merge(lo, hi, 1)

    return list(sort(0, n - 1))


def _exchange(v, i, j):
    v[i], v[j] = jnp.maximum(v[i], v[j]), jnp.minimum(v[i], v[j])


def _top_sorted(x, k, sublanes=8):
    assert x.shape[0] == k * sublanes and k & (k - 1) == 0
    v = [x[i * sublanes:(i + 1) * sublanes, :] for i in range(k)]
    for i, j in _sort_network(k):
        _exchange(v, i, j)
    shift = sublanes // 2
    while shift >= 1:
        other = [pltpu.roll(t, shift, 0) for t in v]
        v = [jnp.maximum(v[i], other[k - 1 - i]) for i in range(k)]
        d = k // 2
        while d >= 1:
            for i in range(k):
                if i & d == 0:
                    _exchange(v, i, i + d)
            d //= 2
        shift //= 2
    return [t[0:1, :] for t in v]


def _candidate_pairs(k):
    return [(a, b) for a in range(k) for b in range(k) if (a + 1) * (b + 1) <= k]


def _peer_route_kernel(xt_ref, wq_ref, sk1_ref, sk2_ref, e1_ref, e2_ref, pt_ref, *, heads, half):
    qt = _dot(wq_ref[...], xt_ref[...]).astype(BF16)
    tm = qt.shape[1]
    pairs = _candidate_pairs(TOPK)
    pad_rows = (-len(pairs)) % 8

    def pair_products(ea, eb):
        pad = [jnp.full((pad_rows, ea[0].shape[1]), -1.0, F32)] if pad_rows else []
        return jnp.concatenate([ea[a] * eb[b] for a, b in pairs] + pad, axis=0)

    for h in range(heads):
        base = h * 2 * half
        s1_all = _dot(sk1_ref[...], qt[base:base + half])
        s2_all = _dot(sk2_ref[...], qt[base + half:base + 2 * half])
        for c0 in range(0, tm, LANES):
            cols = slice(c0, c0 + LANES)
            s1, s2 = s1_all[:, cols], s2_all[:, cols]
            v1, v2 = _top_sorted(s1, TOPK), _top_sorted(s2, TOPK)
            e1 = jnp.exp(s1 - v1[0])
            e2 = jnp.exp(s2 - v2[0])
            e1_top = [jnp.exp(v - v1[0]) for v in v1]
            e2_top = [jnp.exp(v - v2[0]) for v in v2]
            tops = _top_rows(pair_products(e1_top, e2_top), TOPK, -1.0)
            z = functools.reduce(lambda acc, r: acc + jnp.maximum(r, 0.0), tops[1:], tops[0])
            rz = 1.0 / z
            e1n_top = [e * rz for e in e1_top]
            thr = _top_rows(pair_products(e1n_top, e2_top), TOPK, -1.0)[-1]
            e1_ref[h, :, cols] = e1 * rz
            e2_ref[h, :, cols] = e2
            pt_ref[h:h + 1, cols] = jnp.maximum(thr, 0.0)


def _peer_route(x1t, wq_t, sk1, sk2, heads):
    d, n = x1t.shape
    nk, half = sk1.shape
    tm = min(512, n)
    const = lambda i: (0, 0)
    return pl.pallas_call(
        functools.partial(_peer_route_kernel, heads=heads, half=half),
        grid=(n // tm,),
        in_specs=[pl.BlockSpec((d, tm), lambda i: (0, i)),
                  _resident(wq_t.shape, const), _resident(sk1.shape, const), _resident(sk2.shape, const)],
        out_specs=[pl.BlockSpec((heads, nk, tm), lambda i: (0, 0, i)),
                   pl.BlockSpec((heads, nk, tm), lambda i: (0, 0, i)),
                   pl.BlockSpec((heads, tm), lambda i: (0, i))],
        out_shape=[jax.ShapeDtypeStruct((heads, nk, n), F32),
                   jax.ShapeDtypeStruct((heads, nk, n), F32),
                   jax.ShapeDtypeStruct((heads, n), F32)],
        compiler_params=_params("parallel"),
        name="peer_route",
    )(x1t, wq_t, sk1, sk2)


def _peer_dense_kernel(xt_ref, u_ref, vt_ref, e1_ref, e2_ref, pt_ref, x1_ref, g_ref, b_ref, y_ref, acc_ref,
                       *, heads, nk, rows_per_step, alpha):
    j = pl.program_id(1)

    @pl.when(j == 0)
    def _():
        acc_ref[...] = jnp.zeros_like(acc_ref)

    half_rows = u_ref.shape[0] // 2
    act = jnp.concatenate([jax.nn.gelu(_dot(u_ref[r0:r0 + half_rows, :], xt_ref[...]))
                           for r0 in (0, half_rows)], axis=0)
    parts = []
    for a in range(rows_per_step):
        w = None
        for h in range(heads):
            p = e1_ref[h, a:a + 1, :] * e2_ref[h]
            sel = jnp.where(p >= pt_ref[h:h + 1, :], p, 0.0)
            w = sel if w is None else w + sel
        parts.append((w * act[a * nk:(a + 1) * nk]).astype(BF16))
    wa = jnp.concatenate(parts, axis=0) if len(parts) > 1 else parts[0]
    acc_ref[...] += _dot(vt_ref[...], wa)

    @pl.when(j == pl.num_programs(1) - 1)
    def _():
        y_ref[...] = _layer_norm(alpha * x1_ref[...] + acc_ref[...].T, g_ref[...], b_ref[...])


def _peer_dense(x1, x1t, u, vt, e1, e2, pt, ln_g, ln_b, alpha):
    d, n = x1t.shape
    heads, nk, _ = e1.shape
    ne = u.shape[0]
    tm = min(512, n)
    rows_per_step = 8
    ec = rows_per_step * nk
    kern = functools.partial(_peer_dense_kernel, heads=heads, nk=nk, rows_per_step=rows_per_step, alpha=alpha)
    return pl.pallas_call(
        kern,
        grid=(n // tm, ne // ec),
        in_specs=[_resident((d, tm), lambda i, j: (0, i)),
                  pl.BlockSpec((ec, d), lambda i, j: (j, 0)),
                  pl.BlockSpec((d, ec), lambda i, j: (0, j)),
                  pl.BlockSpec((heads, rows_per_step, tm), lambda i, j: (0, j, i)),
                  _resident((heads, nk, tm), lambda i, j: (0, 0, i)),
                  _resident((heads, tm), lambda i, j: (0, i)),
                  _resident((tm, d), lambda i, j: (i, 0)),
                  _resident((1, d), lambda i, j: (0, 0)),
                  _resident((1, d), lambda i, j: (0, 0))],
        out_specs=pl.BlockSpec((tm, d), lambda i, j: (i, 0)),
        out_shape=jax.ShapeDtypeStruct((n, d), F32),
        scratch_shapes=[pltpu.VMEM((d, tm), F32)],
        compiler_params=_params("parallel", "arbitrary"),
        name="peer_dense",
    )(x1t, u, vt, e1, e2, pt, x1, ln_g, ln_b)


def _cast_kernel(w_ref, o_ref, *, transpose):
    w = w_ref[...]
    o_ref[...] = (w.T if transpose else w).astype(o_ref.dtype)


def _cast_table(w3d, layer, *, transpose):
    _, rows, cols = w3d.shape
    tr, tc = min(512, rows), min(2048, cols)
    out_shape = (cols, rows) if transpose else (rows, cols)
    out_block = (tc, tr) if transpose else (tr, tc)
    out_map = (lambda i, j: (j, i)) if transpose else (lambda i, j: (i, j))
    return pl.pallas_call(
        functools.partial(_cast_kernel, transpose=transpose),
        grid=(rows // tr, cols // tc),
        in_specs=[pl.BlockSpec((None, tr, tc), lambda i, j: (layer, i, j))],
        out_specs=pl.BlockSpec(out_block, out_map),
        out_shape=jax.ShapeDtypeStruct(out_shape, BF16),
        compiler_params=_params("parallel", "parallel"),
        name="cast_table",
    )(w3d)


def _prepare_weights(w):
    (w_in, b_in, lambda_q1, lambda_k1, lambda_q2, lambda_k2, subln_g, w_attn_branch,
     conv_w, conv_b, w_rg_a, b_rg_a, w_rg_x, b_rg_x, rg_lambda, w_rec_branch, w_out,
     ln1_g, ln1_b, w_query, sub_keys1, sub_keys2, peer_u, peer_v, ln2_g, ln2_b) = w
    d = w_in.shape[0]
    hd = lambda_q1.shape[0]
    vd = subln_g.shape[0]
    v_w = w_attn_branch.shape[0]
    n_heads = v_w // vd
    qk_w = n_heads * 2 * hd
    row = lambda a: a.reshape(1, -1).astype(F32)
    cuts = (0, qk_w, 2 * qk_w, 2 * qk_w + v_w, w_in.shape[1])
    return dict(
        d=d, hd=hd, vd=vd, n_heads=n_heads, cuts=cuts,
        w_in=w_in.astype(BF16), b_in=row(b_in),
        lam_vec=jnp.stack([lambda_q1, lambda_k1, lambda_q2, lambda_k2]).astype(F32),
        subln_g=row(subln_g),
        w_attn=w_attn_branch.astype(BF16), w_rec=w_rec_branch.astype(BF16), w_out=w_out.astype(BF16),
        conv_w=conv_w.astype(F32), conv_b=row(conv_b),
        w_rg=jnp.concatenate([w_rg_a, w_rg_x], axis=-1).astype(BF16),
        b_rg_a=row(b_rg_a), b_rg_x=row(b_rg_x), rg_lambda=row(rg_lambda),
        ln1_g=row(ln1_g), ln1_b=row(ln1_b), ln2_g=row(ln2_g), ln2_b=row(ln2_b),
        wq_t=w_query.T.astype(BF16), sk1=sub_keys1.astype(BF16), sk2=sub_keys2.astype(BF16),
        peer_heads=w_query.shape[1] // (2 * sub_keys1.shape[1]),
    )


def _layer(x, pos, caches, layer, conv_buf, h0, lam_init, alpha, p):
    b, t, d = x.shape
    n = b * t
    hd, vd, n_heads = p["hd"], p["vd"], p["n_heads"]
    x2d = x.reshape(n, d)
    assert hd == LANES, "the rotary epilogue maps one q/k half onto one 128-lane tile"

    rope = _rope_tables(pos, b, hd)
    cuts, w_in, b_in = p["cuts"], p["w_in"], p["b_in"]
    width = lambda s: cuts[s + 1] - cuts[s]
    q_b, x_b = _project(x2d, w_in, b_in, cuts[0], width(0), rope=rope, out_scale=hd ** -0.5 * LOG2E,
                        emit_f32=False, emit_bf16=True, emit_xb=True, tm=512, tn=width(0))
    k_f, k_b = _project(x_b, w_in, b_in, cuts[1], width(1), rope=rope, emit_bf16=True, split_rows=True,
                        tm=512, tn=width(1))
    v_f, v_b = _project(x_b, w_in, b_in, cuts[2], width(2), emit_bf16=True, tm=512, tn=width(2))
    (zrest,) = _project(x_b, w_in, b_in, cuts[3], width(3), tn=2048)

    q3, k3, v3 = (a.reshape(b, t, -1) for a in (q_b, k_b, v_b))
    if caches is None:
        oa = _attn_prompt(q3, k3, v3, p["lam_vec"], p["subln_g"], lam_init, n_heads, hd)
    else:
        oa = _attn_sample(q3, k3, v3, caches[0], caches[1], layer, p["lam_vec"], p["subln_g"], lam_init,
                          n_heads, hd)

    yb, conv_new, h_last = _recurrent(zrest, b, t, d, conv_buf, h0, p["conv_w"], p["conv_b"], p["w_rg"],
                                      p["b_rg_a"], p["b_rg_x"], p["rg_lambda"])

    x1, x1t = _merge(oa.reshape(n, -1), yb.reshape(n, d), zrest, x2d, p["w_attn"], p["w_rec"], p["w_out"],
                     p["ln1_g"], p["ln1_b"], alpha)

    e1, e2, pt = _peer_route(x1t, p["wq_t"], p["sk1"], p["sk2"], p["peer_heads"])
    y = _peer_dense(x1, x1t, p["peer_u"], p["peer_vt"], e1, e2, pt, p["ln2_g"], p["ln2_b"], alpha)

    return (y.reshape(b, t, d), k_f.reshape(b, t, n_heads, 2, hd), v_f.reshape(b, t, n_heads, vd),
            conv_new, h_last)


def kernel(x_prompt, x_sample, cache_k, cache_v, state_conv, state_h, w_in, b_in, lambda_q1, lambda_k1, lambda_q2, lambda_k2, subln_g, w_attn_branch, conv_w, conv_b, w_rg_a, b_rg_a, w_rg_x, b_rg_x, rg_lambda, w_rec_branch, w_out, ln1_g, ln1_b, w_query, sub_keys1, sub_keys2, peer_u, peer_v, ln2_g, ln2_b):
    layer_weights = (w_in, b_in, lambda_q1, lambda_k1, lambda_q2, lambda_k2, subln_g,
                     w_attn_branch, conv_w, conv_b, w_rg_a, b_rg_a, w_rg_x, b_rg_x, rg_lambda,
                     w_rec_branch, w_out, ln1_g, ln1_b, w_query, sub_keys1, sub_keys2,
                     peer_u, peer_v, ln2_g, ln2_b)
    depth = w_in.shape[0]
    alpha = (2.0 * depth) ** 0.25
    past_len = cache_k.shape[2]
    d_rnn = conv_w.shape[2]
    pos_p = jnp.arange(x_prompt.shape[1])
    pos_s = past_len + jnp.arange(x_sample.shape[1])
    conv0 = jnp.zeros((x_prompt.shape[0], conv_w.shape[1] - 1, d_rnn), x_prompt.dtype)
    h_zero = jnp.zeros((x_prompt.shape[0], d_rnn), x_prompt.dtype)
    caches = (cache_k.reshape(-1, math.prod(cache_k.shape[2:5]), cache_k.shape[5]),
              cache_v.reshape(-1, math.prod(cache_v.shape[2:4]), cache_v.shape[4]))
    yp, ys = x_prompt, x_sample
    outs_p, outs_s = [], []
    for l in range(depth):
        lam_init = 0.8 - 0.6 * math.exp(-0.3 * l)
        p = _prepare_weights(tuple(wt[l] for wt in layer_weights))
        p["peer_u"] = _cast_table(peer_u, l, transpose=False)
        p["peer_vt"] = _cast_table(peer_v, l, transpose=True)
        yp, *rest_p = _layer(yp, pos_p, None, l, conv0, h_zero, lam_init, alpha, p)
        ys, *rest_s = _layer(ys, pos_s, caches, l, state_conv[l], state_h[l], lam_init, alpha, p)
        outs_p.append(rest_p)
        outs_s.append(rest_s)
    stack = lambda outs, idx: jnp.stack([o[idx] for o in outs])
    return (yp, ys, stack(outs_p, 0), stack(outs_p, 1), stack(outs_p, 2), stack(outs_p, 3),
            stack(outs_s, 0), stack(outs_s, 1), stack(outs_s, 2), stack(outs_s, 3))
```

```python
import functools
import math

import jax
import jax.numpy as jnp
from jax import lax
from jax.experimental import pallas as pl
from jax.experimental.pallas import tpu as pltpu

F32 = jnp.float32
BF16 = jnp.bfloat16

CHUNK = 64
ROPE_THETA = 500000.0
RG_C = 8.0
TOPK = 16
LN_EPS = 1e-5
NEG_INF = -1e30
LOG2E = 1.4426950408889634

LANES = 128
VMEM_LIMIT_BYTES = 56 * 1024 * 1024

QKV_ROWS = 512
REST_ROWS = 1024
REST_COLS = 2048
ATTN_BLOCK = 512
ATTN_SOFTMAX_ROWS = 64
SCAN_ROWS = 256
SCAN_UNROLL = 8
MERGE_ROWS = 256
PEER_TOKENS = 512
PEER_SLABS = 8
CAST_ROWS = 512
CAST_COLS = 2048


def _params(*sem):
    return pltpu.CompilerParams(dimension_semantics=sem, vmem_limit_bytes=VMEM_LIMIT_BYTES)


def _resident(shape, index_map):
    return pl.BlockSpec(shape, index_map, pipeline_mode=pl.Buffered(1))


def _dot(a, b):
    return jnp.dot(a, b, preferred_element_type=F32)


def _dot_nt(a, b):
    return lax.dot_general(a, b, (((1,), (1,)), ((), ())), preferred_element_type=F32)


def _proj_kernel(*refs, rope_shift, out_scale, emit_f32, emit_bf16, emit_xb, split_rows):
    it = iter(refs)
    x_ref, w_ref, b_ref = next(it), next(it), next(it)
    if rope_shift:
        c_ref, sa_ref, sb_ref = next(it), next(it), next(it)
    of_ref = next(it) if emit_f32 else None
    ob_ref = next(it) if emit_bf16 else None
    xb_ref = next(it) if emit_xb else None

    x = x_ref[...]
    if x.dtype != BF16:
        x = x.astype(BF16)
    if emit_xb:
        xb_ref[...] = x
    z = _dot(x, w_ref[...]) + b_ref[...]
    n_tiles = z.shape[1] // LANES
    if rope_shift:
        c, sa, sb = c_ref[...], sa_ref[...], sb_ref[...]
        parts = []
        for s in range(n_tiles):
            t = z[:, s * LANES:(s + 1) * LANES]
            parts.append(t * c + pltpu.roll(t, rope_shift, 1) * sa
                         + pltpu.roll(t, LANES - rope_shift, 1) * sb)
        z = jnp.concatenate(parts, axis=1) if len(parts) > 1 else parts[0]
    if out_scale != 1.0:
        z = z * out_scale
    if emit_f32 and split_rows:
        for s in range(n_tiles):
            of_ref[pl.ds(s, z.shape[0], stride=n_tiles), :] = z[:, s * LANES:(s + 1) * LANES]
    elif emit_f32:
        of_ref[...] = z
    if emit_bf16:
        ob_ref[...] = z.astype(BF16)


def _project(x2d, w, b, col0, nout, *, rope=None, out_scale=1.0, emit_f32=True, emit_bf16=False,
             emit_xb=False, split_rows=False, tm, tn):
    n, kdim = x2d.shape
    tm, tn = min(tm, n), math.gcd(min(tn, nout), col0)
    assert not (split_rows and tn != nout)
    jb = col0 // tn
    w_spec = (_resident((kdim, tn), lambda i, j: (0, jb + j)) if tn == nout
              else pl.BlockSpec((kdim, tn), lambda i, j: (0, jb + j)))
    in_specs = [pl.BlockSpec((tm, kdim), lambda i, j: (i, 0)), w_spec,
                pl.BlockSpec((1, tn), lambda i, j: (0, jb + j))]
    args = [x2d, w, b]
    rope_shift = 0
    if rope is not None:
        rope_shift, tabs = rope
        in_specs += [pl.BlockSpec((tm, LANES), lambda i, j: (i, 0))] * 3
        args += list(tabs)
    out_shape, out_specs = [], []
    if emit_f32 and split_rows:
        tiles = nout // LANES
        out_shape.append(jax.ShapeDtypeStruct((n * tiles, LANES), F32))
        out_specs.append(pl.BlockSpec((tm * tiles, LANES), lambda i, j: (i, 0)))
    elif emit_f32:
        out_shape.append(jax.ShapeDtypeStruct((n, nout), F32))
        out_specs.append(pl.BlockSpec((tm, tn), lambda i, j: (i, j)))
    if emit_bf16:
        out_shape.append(jax.ShapeDtypeStruct((n, nout), BF16))
        out_specs.append(pl.BlockSpec((tm, tn), lambda i, j: (i, j)))
    if emit_xb:
        out_shape.append(jax.ShapeDtypeStruct((n, kdim), BF16))
        out_specs.append(pl.BlockSpec((tm, kdim), lambda i, j: (i, 0)))
    return pl.pallas_call(
        functools.partial(_proj_kernel, rope_shift=rope_shift, out_scale=out_scale, emit_f32=emit_f32,
                          emit_bf16=emit_bf16, emit_xb=emit_xb, split_rows=split_rows),
        grid=(n // tm, nout // tn),
        in_specs=in_specs, out_specs=out_specs, out_shape=out_shape,
        compiler_params=_params("parallel", "arbitrary"),
        name="in_proj",
    )(*args)


def _rope_tables(pos, batch, head_dim):
    rope_dim = head_dim // 4
    half = rope_dim // 2
    inv = ROPE_THETA ** (-jnp.arange(half, dtype=F32) / half)
    ang = pos.astype(F32)[:, None] * inv[None, :]
    cos, sin = jnp.cos(ang), jnp.sin(ang)
    t = pos.shape[0]
    ones = jnp.ones((t, head_dim - rope_dim), F32)
    zeros_h = jnp.zeros((t, half), F32)
    zeros_r = jnp.zeros((t, head_dim - rope_dim), F32)
    c = jnp.concatenate([cos, cos, ones], axis=1)
    sa = jnp.concatenate([zeros_h, sin, zeros_r], axis=1)
    sb = jnp.concatenate([-sin, zeros_h, zeros_r], axis=1)
    return half, tuple(jnp.tile(a, (batch, 1)) for a in (c, sa, sb))


def _lambda_value(lam_ref, lam_init):
    lv = lam_ref[...]
    s1 = jnp.sum(lv[0:1] * lv[1:2], axis=1, keepdims=True)
    s2 = jnp.sum(lv[2:3] * lv[3:4], axis=1, keepdims=True)
    return jnp.exp(s1) - jnp.exp(s2) + lam_init


def _sub_norm(o, g, lam_init):
    ms = jnp.mean(o * o, axis=1, keepdims=True)
    return o * lax.rsqrt(ms + LN_EPS) * g * (1.0 - lam_init)


def _attn_prompt_kernel(lam_ref, g_ref, q_ref, k_ref, v_ref, o_ref, s0_ref, s1_ref, p0_ref, p1_ref,
                        a0_ref, a1_ref, acc_ref, m_ref, l_ref, *, tq, hd, lam_init, sub):
    i = pl.program_id(2)
    vd = v_ref.shape[2]
    n_sub = tq // sub
    lane_tiles = tq // LANES
    s_refs, p_refs, a_refs = (s0_ref, s1_ref), (p0_ref, p1_ref), (a0_ref, a1_ref)

    def scores(j, par):
        off = pl.multiple_of(j * tq, tq)
        for comp in range(2):
            s_refs[par][comp] = _dot_nt(q_ref[0, :, comp * hd:(comp + 1) * hd],
                                        k_ref[0, pl.ds(off, tq), comp * hd:(comp + 1) * hd])

    def softmax(par, masked):
        for comp in range(2):
            for r in range(n_sub):
                rows = slice(r * sub, (r + 1) * sub)
                s = s_refs[par][comp, rows, :]
                if masked:
                    q_chunk = (lax.broadcasted_iota(jnp.int32, (sub, tq), 0) + r * sub) // CHUNK
                    k_chunk = lax.broadcasted_iota(jnp.int32, (sub, tq), 1) // CHUNK
                    s = jnp.where(k_chunk <= q_chunk, s, NEG_INF)
                m_prev = m_ref[comp, rows, :]
                m_next = jnp.maximum(m_prev, jnp.max(s, axis=1, keepdims=True))
                alpha = jnp.exp2(m_prev - m_next)
                p = jnp.exp2(s - jnp.tile(m_next, (1, lane_tiles)))
                psum = p[:, 0:LANES]
                for t in range(1, lane_tiles):
                    psum = psum + p[:, t * LANES:(t + 1) * LANES]
                l_ref[comp, rows, :] = alpha * l_ref[comp, rows, :] + psum
                m_ref[comp, rows, :] = m_next
                a_refs[par][comp, rows, :] = alpha
                p_refs[par][comp, rows, :] = p.astype(BF16)

    def accumulate(j, par):
        off = pl.multiple_of(j * tq, tq)
        v = v_ref[0, pl.ds(off, tq), :]
        pv = _dot(p_refs[par][...].reshape(2 * tq, tq), v)
        for comp in range(2):
            acc_ref[comp] = (acc_ref[comp] * jnp.tile(a_refs[par][comp], (1, vd // LANES))
                             + pv[comp * tq:(comp + 1) * tq])

    def pipelined_step(j, par):
        accumulate(j - 1, 1 - par)
        softmax(par, False)
        scores(j + 1, 1 - par)

    acc_ref[...] = jnp.zeros_like(acc_ref)
    m_ref[...] = jnp.full_like(m_ref, NEG_INF)
    l_ref[...] = jnp.zeros_like(l_ref)

    scores(0, 0)

    @pl.when(i > 0)
    def _():
        softmax(0, False)
        scores(1, 1)

    def pair(t, carry):
        pipelined_step(2 * t + 1, 1)
        pipelined_step(2 * t + 2, 0)
        return carry

    lax.fori_loop(0, (i - 1) // 2, pair, 0)

    @pl.when(jnp.logical_and(i > 1, i % 2 == 0))
    def _():
        pipelined_step(i - 1, 1)

    @pl.when(i % 2 == 0)
    def _():
        @pl.when(i > 0)
        def _():
            accumulate(i - 1, 1)
        softmax(0, True)
        accumulate(i, 0)

    @pl.when(i % 2 == 1)
    def _():
        accumulate(i - 1, 0)
        softmax(1, True)
        accumulate(i, 1)

    lam = _lambda_value(lam_ref, lam_init)
    l1 = jnp.sum(l_ref[0], axis=1, keepdims=True)
    l2 = jnp.sum(l_ref[1], axis=1, keepdims=True)
    o = acc_ref[0] / l1 - lam * (acc_ref[1] / l2)
    o_ref[0] = _sub_norm(o, g_ref[...], lam_init).astype(o_ref.dtype)


def _attn_prompt(q, k, v, lam_vec, g, lam_init, n_heads, hd):
    b, t, _ = q.shape
    vd = v.shape[2] // n_heads
    tq = min(ATTN_BLOCK, t)
    sub = min(ATTN_SOFTMAX_ROWS, tq)
    kern = functools.partial(_attn_prompt_kernel, tq=tq, hd=hd, lam_init=lam_init, sub=sub)
    two = lambda shape, dt: [pltpu.VMEM(shape, dt), pltpu.VMEM(shape, dt)]
    return pl.pallas_call(
        kern,
        grid=(b, n_heads, t // tq),
        in_specs=[_resident((4, hd), lambda bi, h, i: (0, 0)),
                  _resident((1, vd), lambda bi, h, i: (0, 0)),
                  pl.BlockSpec((1, tq, 2 * hd), lambda bi, h, i: (bi, i, h)),
                  pl.BlockSpec((1, t, 2 * hd), lambda bi, h, i: (bi, 0, h)),
                  pl.BlockSpec((1, t, vd), lambda bi, h, i: (bi, 0, h))],
        out_specs=pl.BlockSpec((1, tq, vd), lambda bi, h, i: (bi, i, h)),
        out_shape=jax.ShapeDtypeStruct((b, t, n_heads * vd), BF16),
        scratch_shapes=(two((2, tq, tq), F32) + two((2, tq, tq), BF16) + two((2, tq, LANES), F32)
                        + [pltpu.VMEM((2, tq, vd), F32), pltpu.VMEM((2, tq, LANES), F32),
                           pltpu.VMEM((2, tq, LANES), F32)]),
        compiler_params=_params("parallel", "parallel", "arbitrary"),
        name="attn_prompt",
    )(lam_vec, g, q, k, v)


def _attn_sample_kernel(lam_ref, g_ref, q_ref, kn_ref, vn_ref, kp_ref, *rest, n_heads, hd, lam_init):
    vp_refs, o_ref = rest[:-1], rest[-1]
    past = vp_refs[0].shape[1] // n_heads
    vd = LANES * len(vp_refs)
    lam = _lambda_value(lam_ref, lam_init)
    for h in range(n_heads):
        vp = jnp.concatenate([r[0, pl.ds(h, past, stride=n_heads), :].astype(BF16) for r in vp_refs],
                             axis=1)
        vn = vn_ref[0, :, h * vd:(h + 1) * vd]
        outs = []
        for comp in range(2):
            col = (2 * h + comp) * hd
            q = q_ref[0, :, col:col + hd]
            kp = kp_ref[0, pl.ds(2 * h + comp, past, stride=2 * n_heads), :].astype(BF16)
            sp = _dot_nt(q, kp)
            sn = _dot_nt(q, kn_ref[0, :, col:col + hd])
            m = jnp.maximum(jnp.max(sp, axis=1, keepdims=True), jnp.max(sn, axis=1, keepdims=True))
            pp = jnp.exp2(sp - m)
            pn = jnp.exp2(sn - m)
            l = jnp.sum(pp, axis=1, keepdims=True) + jnp.sum(pn, axis=1, keepdims=True)
            outs.append((_dot(pp.astype(BF16), vp) + _dot(pn.astype(BF16), vn)) / l)
        o = outs[0] - lam * outs[1]
        o_ref[0, :, h * vd:(h + 1) * vd] = _sub_norm(o, g_ref[...], lam_init).astype(o_ref.dtype)


def _attn_sample(q, k, v, k_past, v_past, layer, lam_vec, g, lam_init, n_heads, hd):
    b, t, _ = q.shape
    vd = v.shape[2] // n_heads
    kern = functools.partial(_attn_sample_kernel, n_heads=n_heads, hd=hd, lam_init=lam_init)
    tok = lambda w: pl.BlockSpec((1, t, w), lambda bi: (bi, 0, 0))
    return pl.pallas_call(
        kern,
        grid=(b,),
        in_specs=[_resident((4, hd), lambda bi: (0, 0)),
                  _resident((1, vd), lambda bi: (0, 0)),
                  tok(q.shape[2]), tok(k.shape[2]), tok(v.shape[2]),
                  pl.BlockSpec((1,) + k_past.shape[1:], lambda bi: (layer * b + bi, 0, 0))]
                 + [pl.BlockSpec((1, v_past.shape[1], LANES), lambda bi, c=c: (layer * b + bi, 0, c))
                    for c in range(vd // LANES)],
        out_specs=tok(n_heads * vd),
        out_shape=jax.ShapeDtypeStruct((b, t, n_heads * vd), BF16),
        compiler_params=_params("parallel"),
        name="attn_sample",
    )(lam_vec, g, q, k, v, k_past, *([v_past] * (vd // LANES)))


def _recurrent_kernel(xr_ref, xg_ref, cbuf_ref, h0_ref, cw_ref, cb_ref, wrg_ref, bra_ref, brx_ref,
                      lam_ref, y_ref, cnew_ref, hlast_ref, xpad_ref, a_ref, u_ref, hs_ref, hc_ref,
                      *, tt, cw, bw):
    i = pl.program_id(1)
    hdr = 8
    d = xr_ref.shape[2]

    @pl.when(i == 0)
    def _():
        xpad_ref[hdr - (cw - 1):hdr, :] = cbuf_ref[0]
        hc_ref[...] = h0_ref[0]

    xpad_ref[hdr:hdr + tt, :] = xr_ref[0]
    xc = cb_ref[...] + xpad_ref[hdr - (cw - 1):hdr - (cw - 1) + tt, :] * cw_ref[0:1, :]
    for j in range(1, cw):
        lo = hdr - (cw - 1) + j
        xc = xc + xpad_ref[lo:lo + tt, :] * cw_ref[j:j + 1, :]
    tail = xpad_ref[hdr + tt - (cw - 1):hdr + tt, :]
    cnew_ref[0] = tail
    xpad_ref[hdr - (cw - 1):hdr, :] = tail

    for n in range(d // bw):
        lo, hi = n * bw, (n + 1) * bw
        xcn = xc[:, lo:hi]
        gates = _dot(xcn.astype(BF16), wrg_ref[n])
        r = jax.nn.sigmoid(gates[:, :bw] + bra_ref[:, lo:hi])
        gi = jax.nn.sigmoid(gates[:, bw:] + brx_ref[:, lo:hi])
        log_a = -RG_C * r * jax.nn.softplus(-lam_ref[:, lo:hi])
        a = jnp.exp(log_a)
        a_ref[:, lo:hi] = a
        u_ref[:, lo:hi] = jnp.sqrt(-jnp.tanh(log_a) * (a * a + 1.0)) * (gi * xcn)

    def scan_row(t, h):
        h = a_ref[pl.ds(t, 1), :] * h + u_ref[pl.ds(t, 1), :]
        hs_ref[pl.ds(t, 1), :] = h
        return h

    h = lax.fori_loop(0, tt, scan_row, hc_ref[...], unroll=math.gcd(SCAN_UNROLL, tt))
    hc_ref[...] = h
    hlast_ref[0] = h
    y_ref[0] = (hs_ref[...] * jax.nn.gelu(xg_ref[0])).astype(y_ref.dtype)


def _recurrent(zrest, b, t, d, conv_buf, h0, conv_w, conv_b, w_rg, b_rg_a, b_rg_x, rg_lambda):
    cw = conv_w.shape[0]
    bw = w_rg.shape[1]
    tt = min(SCAN_ROWS, t)
    z3 = zrest.reshape(b, t, zrest.shape[1])
    kern = functools.partial(_recurrent_kernel, tt=tt, cw=cw, bw=bw)
    row = lambda bi, i: (0, 0)
    y, cnew, hlast = pl.pallas_call(
        kern,
        grid=(b, t // tt),
        in_specs=[pl.BlockSpec((1, tt, d), lambda bi, i: (bi, i, 0)),
                  pl.BlockSpec((1, tt, d), lambda bi, i: (bi, i, 1)),
                  pl.BlockSpec((1, cw - 1, d), lambda bi, i: (bi, 0, 0)),
                  pl.BlockSpec((1, 1, d), lambda bi, i: (bi, 0, 0)),
                  _resident((cw, d), row),
                  _resident((1, d), row),
                  _resident(w_rg.shape, lambda bi, i: (0, 0, 0)),
                  _resident((1, d), row),
                  _resident((1, d), row),
                  _resident((1, d), row)],
        out_specs=[pl.BlockSpec((1, tt, d), lambda bi, i: (bi, i, 0)),
                   pl.BlockSpec((1, cw - 1, d), lambda bi, i: (bi, 0, 0)),
                   pl.BlockSpec((1, 1, d), lambda bi, i: (bi, 0, 0))],
        out_shape=[jax.ShapeDtypeStruct((b, t, d), BF16),
                   jax.ShapeDtypeStruct((b, cw - 1, d), F32),
                   jax.ShapeDtypeStruct((b, 1, d), F32)],
        scratch_shapes=[pltpu.VMEM((tt + 8, d), F32), pltpu.VMEM((tt, d), F32),
                        pltpu.VMEM((tt, d), F32), pltpu.VMEM((tt, d), F32), pltpu.VMEM((1, d), F32)],
        compiler_params=_params("parallel", "arbitrary"),
        name="recurrent",
    )(z3, z3, conv_buf, h0.reshape(b, 1, d), conv_w, conv_b, w_rg, b_rg_a, b_rg_x, rg_lambda)
    return y, cnew, hlast.reshape(b, d)


def _layer_norm(x, g, b):
    mu = jnp.mean(x, axis=1, keepdims=True)
    xc = x - mu
    var = jnp.mean(xc * xc, axis=1, keepdims=True)
    return xc * lax.rsqrt(var + LN_EPS) * g + b


def _merge_kernel(oa_ref, yb_ref, ga_ref, gr_ref, x_ref, wa_ref, wb_ref, wo_ref, g_ref, b_ref,
                  x1_ref, x1t_ref, *, alpha):
    br_a = _dot(oa_ref[...], wa_ref[...])
    br_b = _dot(yb_ref[...], wb_ref[...])
    m = jax.nn.sigmoid(ga_ref[...]) * br_a + jax.nn.sigmoid(gr_ref[...]) * br_b
    r = _dot(m.astype(BF16), wo_ref[...])
    x1 = _layer_norm(alpha * x_ref[...] + r, g_ref[...], b_ref[...])
    x1_ref[...] = x1
    x1t_ref[...] = x1.T.astype(BF16)


def _merge(oa, yb, zrest, x2d, w_attn, w_rec, w_out, ln_g, ln_b, alpha):
    n, d = x2d.shape
    tm = min(MERGE_ROWS, n)
    tok = lambda c: pl.BlockSpec((tm, d), lambda i: (i, c))
    const = lambda i: (0, 0)
    return pl.pallas_call(
        functools.partial(_merge_kernel, alpha=alpha),
        grid=(n // tm,),
        in_specs=[pl.BlockSpec((tm, oa.shape[1]), lambda i: (i, 0)), tok(0), tok(2), tok(3), tok(0),
                  _resident(w_attn.shape, const), _resident(w_rec.shape, const),
                  _resident(w_out.shape, const), _resident((1, d), const), _resident((1, d), const)],
        out_specs=[tok(0), pl.BlockSpec((d, tm), lambda i: (0, i))],
        out_shape=[jax.ShapeDtypeStruct((n, d), F32), jax.ShapeDtypeStruct((d, n), BF16)],
        compiler_params=_params("parallel"),
        name="merge",
    )(oa, yb, zrest, zrest, x2d, w_attn, w_rec, w_out, ln_g, ln_b)


def _top_rows(x, k, fill):
    rows = []
    for it in range(k):
        m = jnp.max(x, axis=0, keepdims=True)
        rows.append(m)
        if it + 1 < k:
            x = jnp.where(x == m, fill, x)
    return rows


def _sort_network(n):
    def merge(lo, hi, r):
        step = r * 2
        if step < hi - lo:
            yield from merge(lo, hi, step)
            yield from merge(lo + r, hi, step)
            yield from ((i, i + r) for i in range(lo + r, hi - r, step))
        else:
            yield (lo, lo + r)

    def sort(lo, hi):
        if hi > lo:
            mid = lo + (hi - lo) // 2
            yield from sort(lo, mid)
            yield from sort(mid + 1, hi)
            yield from merge(lo, hi, 1)

    return list(sort(0, n - 1))


def _exchange(v, i, j):
    v[i], v[j] = jnp.maximum(v[i], v[j]), jnp.minimum(v[i], v[j])


def _top_sorted(x, k, sublanes=8):
    assert x.shape[0] == k * sublanes and k & (k - 1) == 0
    v = [x[i * sublanes:(i + 1) * sublanes, :] for i in range(k)]
    for i, j in _sort_network(k):
        _exchange(v, i, j)
    shift = sublanes // 2
    while shift >= 1:
        other = [pltpu.roll(t, shift, 0) for t in v]
        v = [jnp.maximum(v[i], other[k - 1 - i]) for i in range(k)]
        d = k // 2
        while d >= 1:
            for i in range(k):
                if i & d == 0:
                    _exchange(v, i, i + d)
            d //= 2
        shift //= 2
    return [t[0:1, :] for t in v]


def _candidate_pairs(k):
    return [(a, b) for a in range(k) for b in range(k) if (a + 1) * (b + 1) <= k]


def _peer_route_kernel(xt_ref, wq_ref, sk1_ref, sk2_ref, e1_ref, e2_ref, pt_ref, *, heads, half):
    qt = _dot(wq_ref[...], xt_ref[...]).astype(BF16)
    tm = qt.shape[1]
    pairs = _candidate_pairs(TOPK)
    pad_rows = (-len(pairs)) % 8

    def pair_products(ea, eb):
        pad = [jnp.full((pad_rows, ea[0].shape[1]), -1.0, F32)] if pad_rows else []
        return jnp.concatenate([ea[a] * eb[b] for a, b in pairs] + pad, axis=0)

    for h in range(heads):
        base = h * 2 * half
        s1_all = _dot(sk1_ref[...], qt[base:base + half])
        s2_all = _dot(sk2_ref[...], qt[base + half:base + 2 * half])
        for c0 in range(0, tm, LANES):
            cols = slice(c0, c0 + LANES)
            s1, s2 = s1_all[:, cols], s2_all[:, cols]
            v1, v2 = _top_sorted(s1, TOPK), _top_sorted(s2, TOPK)
            e1 = jnp.exp(s1 - v1[0])
            e2 = jnp.exp(s2 - v2[0])
            e1_top = [jnp.exp(v - v1[0]) for v in v1]
            e2_top = [jnp.exp(v - v2[0]) for v in v2]
            tops = _top_rows(pair_products(e1_top, e2_top), TOPK, -1.0)
            z = functools.reduce(lambda acc, r: acc + jnp.maximum(r, 0.0), tops[1:], tops[0])
            rz = 1.0 / z
            e1n_top = [e * rz for e in e1_top]
            thr = _top_rows(pair_products(e1n_top, e2_top), TOPK, -1.0)[-1]
            e1_ref[h, :, cols] = e1 * rz
            e2_ref[h, :, cols] = e2
            pt_ref[h:h + 1, cols] = jnp.maximum(thr, 0.0)


def _peer_route(x1t, wq_t, sk1, sk2, heads):
    d, n = x1t.shape
    nk, half = sk1.shape
    tm = min(PEER_TOKENS, n)
    const = lambda i: (0, 0)
    return pl.pallas_call(
        functools.partial(_peer_route_kernel, heads=heads, half=half),
        grid=(n // tm,),
        in_specs=[pl.BlockSpec((d, tm), lambda i: (0, i)),
                  _resident(wq_t.shape, const), _resident(sk1.shape, const), _resident(sk2.shape, const)],
        out_specs=[pl.BlockSpec((heads, nk, tm), lambda i: (0, 0, i)),
                   pl.BlockSpec((heads, nk, tm), lambda i: (0, 0, i)),
                   pl.BlockSpec((heads, tm), lambda i: (0, i))],
        out_shape=[jax.ShapeDtypeStruct((heads, nk, n), F32),
                   jax.ShapeDtypeStruct((heads, nk, n), F32),
                   jax.ShapeDtypeStruct((heads, n), F32)],
        compiler_params=_params("parallel"),
        name="peer_route",
    )(x1t, wq_t, sk1, sk2)


def _peer_dense_kernel(xt_ref, u_ref, vt_ref, e1_ref, e2_ref, pt_ref, x1_ref, g_ref, b_ref, y_ref, acc_ref,
                       *, heads, nk, rows_per_step, alpha):
    j = pl.program_id(1)

    @pl.when(j == 0)
    def _():
        acc_ref[...] = jnp.zeros_like(acc_ref)

    half_rows = u_ref.shape[0] // 2
    act = jnp.concatenate([jax.nn.gelu(_dot(u_ref[r0:r0 + half_rows, :], xt_ref[...]))
                           for r0 in (0, half_rows)], axis=0)
    parts = []
    for a in range(rows_per_step):
        w = None
        for h in range(heads):
            p = e1_ref[h, a:a + 1, :] * e2_ref[h]
            sel = jnp.where(p >= pt_ref[h:h + 1, :], p, 0.0)
            w = sel if w is None else w + sel
        parts.append((w * act[a * nk:(a + 1) * nk]).astype(BF16))
    wa = jnp.concatenate(parts, axis=0) if len(parts) > 1 else parts[0]
    acc_ref[...] += _dot(vt_ref[...], wa)

    @pl.when(j == pl.num_programs(1) - 1)
    def _():
        y_ref[...] = _layer_norm(alpha * x1_ref[...] + acc_ref[...].T, g_ref[...], b_ref[...])


def _peer_dense(x1, x1t, u, vt, e1, e2, pt, ln_g, ln_b, alpha):
    d, n = x1t.shape
    heads, nk, _ = e1.shape
    ne = u.shape[0]
    tm = min(PEER_TOKENS, n)
    rows_per_step = PEER_SLABS
    ec = rows_per_step * nk
    kern = functools.partial(_peer_dense_kernel, heads=heads, nk=nk, rows_per_step=rows_per_step, alpha=alpha)
    return pl.pallas_call(
        kern,
        grid=(n // tm, ne // ec),
        in_specs=[_resident((d, tm), lambda i, j: (0, i)),
                  pl.BlockSpec((ec, d), lambda i, j: (j, 0)),
                  pl.BlockSpec((d, ec), lambda i, j: (0, j)),
                  pl.BlockSpec((heads, rows_per_step, tm), lambda i, j: (0, j, i)),
                  _resident((heads, nk, tm), lambda i, j: (0, 0, i)),
                  _resident((heads, tm), lambda i, j: (0, i)),
                  _resident((tm, d), lambda i, j: (i, 0)),
                  _resident((1, d), lambda i, j: (0, 0)),
                  _resident((1, d), lambda i, j: (0, 0))],
        out_specs=pl.BlockSpec((tm, d), lambda i, j: (i, 0)),
        out_shape=jax.ShapeDtypeStruct((n, d), F32),
        scratch_shapes=[pltpu.VMEM((d, tm), F32)],
        compiler_params=_params("parallel", "arbitrary"),
        name="peer_dense",
    )(x1t, u, vt, e1, e2, pt, x1, ln_g, ln_b)


def _cast_kernel(w_ref, o_ref, *, transpose):
    w = w_ref[...]
    o_ref[...] = (w.T if transpose else w).astype(o_ref.dtype)


def _cast_table(w3d, layer, *, transpose):
    _, rows, cols = w3d.shape
    tr, tc = min(CAST_ROWS, rows), min(CAST_COLS, cols)
    out_shape = (cols, rows) if transpose else (rows, cols)
    out_block = (tc, tr) if transpose else (tr, tc)
    out_map = (lambda i, j: (j, i)) if transpose else (lambda i, j: (i, j))
    return pl.pallas_call(
        functools.partial(_cast_kernel, transpose=transpose),
        grid=(rows // tr, cols // tc),
        in_specs=[pl.BlockSpec((None, tr, tc), lambda i, j: (layer, i, j))],
        out_specs=pl.BlockSpec(out_block, out_map),
        out_shape=jax.ShapeDtypeStruct(out_shape, BF16),
        compiler_params=_params("parallel", "parallel"),
        name="cast_table",
    )(w3d)


def _prepare_weights(w):
    (w_in, b_in, lambda_q1, lambda_k1, lambda_q2, lambda_k2, subln_g, w_attn_branch,
     conv_w, conv_b, w_rg_a, b_rg_a, w_rg_x, b_rg_x, rg_lambda, w_rec_branch, w_out,
     ln1_g, ln1_b, w_query, sub_keys1, sub_keys2, peer_u, peer_v, ln2_g, ln2_b) = w
    d = w_in.shape[0]
    hd = lambda_q1.shape[0]
    vd = subln_g.shape[0]
    v_w = w_attn_branch.shape[0]
    n_heads = v_w // vd
    qk_w = n_heads * 2 * hd
    row = lambda a: a.reshape(1, -1).astype(F32)
    cuts = (0, qk_w, 2 * qk_w, 2 * qk_w + v_w, w_in.shape[1])
    return dict(
        d=d, hd=hd, vd=vd, n_heads=n_heads, cuts=cuts,
        w_in=w_in.astype(BF16), b_in=row(b_in),
        lam_vec=jnp.stack([lambda_q1, lambda_k1, lambda_q2, lambda_k2]).astype(F32),
        subln_g=row(subln_g),
        w_attn=w_attn_branch.astype(BF16), w_rec=w_rec_branch.astype(BF16), w_out=w_out.astype(BF16),
        conv_w=conv_w.astype(F32), conv_b=row(conv_b),
        w_rg=jnp.concatenate([w_rg_a, w_rg_x], axis=-1).astype(BF16),
        b_rg_a=row(b_rg_a), b_rg_x=row(b_rg_x), rg_lambda=row(rg_lambda),
        ln1_g=row(ln1_g), ln1_b=row(ln1_b), ln2_g=row(ln2_g), ln2_b=row(ln2_b),
        wq_t=w_query.T.astype(BF16), sk1=sub_keys1.astype(BF16), sk2=sub_keys2.astype(BF16),
        peer_heads=w_query.shape[1] // (2 * sub_keys1.shape[1]),
    )


def _layer(x, pos, caches, layer, conv_buf, h0, lam_init, alpha, p):
    b, t, d = x.shape
    n = b * t
    hd, vd, n_heads = p["hd"], p["vd"], p["n_heads"]
    x2d = x.reshape(n, d)
    assert hd == LANES, "the rotary epilogue maps one q/k half onto one 128-lane tile"

    rope = _rope_tables(pos, b, hd)
    cuts, w_in, b_in = p["cuts"], p["w_in"], p["b_in"]
    width = lambda s: cuts[s + 1] - cuts[s]
    q_b, x_b = _project(x2d, w_in, b_in, cuts[0], width(0), rope=rope, out_scale=hd ** -0.5 * LOG2E,
                        emit_f32=False, emit_bf16=True, emit_xb=True, tm=QKV_ROWS, tn=width(0))
    k_f, k_b = _project(x_b, w_in, b_in, cuts[1], width(1), rope=rope, emit_bf16=True, split_rows=True,
                        tm=QKV_ROWS, tn=width(1))
    v_f, v_b = _project(x_b, w_in, b_in, cuts[2], width(2), emit_bf16=True, tm=QKV_ROWS, tn=width(2))
    (zrest,) = _project(x_b, w_in, b_in, cuts[3], width(3), tm=REST_ROWS, tn=REST_COLS)

    q3, k3, v3 = (a.reshape(b, t, -1) for a in (q_b, k_b, v_b))
    if caches is None:
        oa = _attn_prompt(q3, k3, v3, p["lam_vec"], p["subln_g"], lam_init, n_heads, hd)
    else:
        oa = _attn_sample(q3, k3, v3, caches[0], caches[1], layer, p["lam_vec"], p["subln_g"], lam_init,
                          n_heads, hd)

    yb, conv_new, h_last = _recurrent(zrest, b, t, d, conv_buf, h0, p["conv_w"], p["conv_b"], p["w_rg"],
                                      p["b_rg_a"], p["b_rg_x"], p["rg_lambda"])

    x1, x1t = _merge(oa.reshape(n, -1), yb.reshape(n, d), zrest, x2d, p["w_attn"], p["w_rec"], p["w_out"],
                     p["ln1_g"], p["ln1_b"], alpha)

    e1, e2, pt = _peer_route(x1t, p["wq_t"], p["sk1"], p["sk2"], p["peer_heads"])
    y = _peer_dense(x1, x1t, p["peer_u"], p["peer_vt"], e1, e2, pt, p["ln2_g"], p["ln2_b"], alpha)

    return (y.reshape(b, t, d), k_f.reshape(b, t, n_heads, 2, hd), v_f.reshape(b, t, n_heads, vd),
            conv_new, h_last)


def kernel(x_prompt, x_sample, cache_k, cache_v, state_conv, state_h, w_in, b_in, lambda_q1, lambda_k1, lambda_q2, lambda_k2, subln_g, w_attn_branch, conv_w, conv_b, w_rg_a, b_rg_a, w_rg_x, b_rg_x, rg_lambda, w_rec_branch, w_out, ln1_g, ln1_b, w_query, sub_keys1, sub_keys2, peer_u, peer_v, ln2_g, ln2_b):
    layer_weights = (w_in, b_in, lambda_q1, lambda_k1, lambda_q2, lambda_k2, subln_g,
                     w_attn_branch, conv_w, conv_b, w_rg_a, b_rg_a, w_rg_x, b_rg_x, rg_lambda,
                     w_rec_branch, w_out, ln1_g, ln1_b, w_query, sub_keys1, sub_keys2,
                     peer_u, peer_v, ln2_g, ln2_b)
    depth = w_in.shape[0]
    alpha = (2.0 * depth) ** 0.25
    past_len = cache_k.shape[2]
    d_rnn = conv_w.shape[2]
    pos_p = jnp.arange(x_prompt.shape[1])
    pos_s = past_len + jnp.arange(x_sample.shape[1])
    conv0 = jnp.zeros((x_prompt.shape[0], conv_w.shape[1] - 1, d_rnn), x_prompt.dtype)
    h_zero = jnp.zeros((x_prompt.shape[0], d_rnn), x_prompt.dtype)
    caches = (cache_k.reshape(-1, math.prod(cache_k.shape[2:5]), cache_k.shape[5]),
              cache_v.reshape(-1, math.prod(cache_v.shape[2:4]), cache_v.shape[4]))
    yp, ys = x_prompt, x_sample
    outs_p, outs_s = [], []
    for l in range(depth):
        lam_init = 0.8 - 0.6 * math.exp(-0.3 * l)
        p = _prepare_weights(tuple(wt[l] for wt in layer_weights))
        p["peer_u"] = _cast_table(peer_u, l, transpose=False)
        p["peer_vt"] = _cast_table(peer_v, l, transpose=True)
        yp, *rest_p = _layer(yp, pos_p, None, l, conv0, h_zero, lam_init, alpha, p)
        ys, *rest_s = _layer(ys, pos_s, caches, l, state_conv[l], state_h[l], lam_init, alpha, p)
        outs_p.append(rest_p)
        outs_s.append(rest_s)
    stack = lambda outs, idx: jnp.stack([o[idx] for o in outs])
    return (yp, ys, stack(outs_p, 0), stack(outs_p, 1), stack(outs_p, 2), stack(outs_p, 3),
            stack(outs_s, 0), stack(outs_s, 1), stack(outs_s, 2), stack(outs_s, 3))
```
